```python
import jax, jax.numpy as jnp
from jax import lax
import numpy as np

D_MODEL = 2048
BATCH = 1
SEQ = 8192
DEPTH = 4

CONV_CHANNELS = D_MODEL
CONV_WIDTH = 31
HEAD_DIM_K = 128
HEAD_DIM_V = 128
N_HEADS = D_MODEL // 128
KEY_DIM = N_HEADS * HEAD_DIM_K
VALUE_DIM = N_HEADS * HEAD_DIM_V
SHORT_CONV = 4
CHUNK = 64
D_FF = -((-8 * D_MODEL) // (3 * 256)) * 256
DEEPNORM_ALPHA = (2.0 * DEPTH) ** 0.25
DEEPNORM_BETA = (8.0 * DEPTH) ** -0.25
LN_EPS = 1e-5
IN_SPLITS = (CONV_CHANNELS, CONV_CHANNELS, 2 * KEY_DIM + VALUE_DIM, VALUE_DIM,
             N_HEADS, N_HEADS, D_MODEL, D_MODEL)
IN_WIDTH = sum(IN_SPLITS)

kernel_name = "hybrid_conformer_gdn_deepnorm"


def layer_norm(x, g, b):
    xf = x.astype(jnp.float32)
    mu = jnp.mean(xf, axis=-1, keepdims=True)
    var = jnp.mean(jnp.square(xf - mu), axis=-1, keepdims=True)
    y = (xf - mu) * lax.rsqrt(var + LN_EPS) * g.astype(jnp.float32) + b.astype(jnp.float32)
    return y.astype(x.dtype)


def causal_depthwise_conv(x, w):
    k, c = w.shape
    return lax.conv_general_dilated(
        x, w[:, None, :], window_strides=(1,), padding=[(k - 1, 0)],
        dimension_numbers=('NWC', 'WIO', 'NWC'), feature_group_count=c)


def l2_normalize(x):
    return x * lax.rsqrt(jnp.sum(jnp.square(x), axis=-1, keepdims=True) + 1e-6)


def gated_delta_rule(q, k, v, beta, log_a):
    bsz, seq, h, dk = q.shape
    dv = v.shape[-1]
    n = seq // CHUNK

    def chunks(t):
        t = t.reshape((bsz, n, CHUNK, h) + t.shape[3:])
        return jnp.moveaxis(t, 3, 1)

    q = chunks(q) * (dk ** -0.5)
    k, v, beta = chunks(k), chunks(v), chunks(beta)
    g = jnp.cumsum(chunks(log_a), axis=-1)
    causal = jnp.tril(jnp.ones((CHUNK, CHUNK), dtype=bool))
    strict = jnp.tril(jnp.ones((CHUNK, CHUNK), dtype=bool), -1)
    decay = jnp.exp(jnp.where(causal, g[..., :, None] - g[..., None, :], -jnp.inf))

    kb = k * beta[..., None]
    kkt = jnp.einsum('bhnid,bhnjd->bhnij', kb, k) * decay
    a_mat = jnp.eye(CHUNK, dtype=jnp.float32) + jnp.where(strict, kkt, 0.0)
    rhs = jnp.concatenate([v * beta[..., None], kb * jnp.exp(g)[..., None]], axis=-1)
    sol = lax.linalg.triangular_solve(a_mat, rhs, left_side=True, lower=True,
                                      unit_diagonal=True)
    u = sol[..., :dv]
    w = sol[..., dv:]

    attn_intra = jnp.where(causal, jnp.einsum('bhnid,bhnjd->bhnij', q, k) * decay, 0.0)
    q_dec = q * jnp.exp(g)[..., None]
    g_last = g[..., -1]
    k_dec = k * jnp.exp(g_last[..., None] - g)[..., None]

    def step(state, inp):
        u_c, w_c, qd_c, kd_c, at_c, gl_c = inp
        v_new = u_c - jnp.einsum('bhck,bhkv->bhcv', w_c, state)
        o = jnp.einsum('bhck,bhkv->bhcv', qd_c, state) + jnp.einsum('bhcj,bhjv->bhcv', at_c, v_new)
        state = state * jnp.exp(gl_c)[..., None, None] + jnp.einsum('bhck,bhcv->bhkv', kd_c, v_new)
        return state, o

    xs = tuple(jnp.moveaxis(t, 2, 0) for t in (u, w, q_dec, k_dec, attn_intra, g_last))
    s0 = jnp.zeros((bsz, h, dk, dv), dtype=jnp.float32)
    _, o = lax.scan(step, s0, xs)
    return jnp.transpose(o, (1, 0, 3, 2, 4)).reshape(bsz, seq, h, dv)


def mixer(h, w_in, b_in, conv_dw_w, conv_dw_b, conv_ln_g, conv_ln_b, w_conv_proj, b_conv_proj,
          short_conv_w, a_log, dt_bias, gdn_norm_w, w_gdn_proj, w_out):
    bsz, seq, _ = h.shape
    z = h @ w_in + b_in
    offsets = np.cumsum(IN_SPLITS)[:-1].tolist()
    glu_a, glu_b, qkv, zgate, beta_raw, a_raw, gate_a, gate_b = jnp.split(z, offsets, axis=-1)

    c = glu_a * jax.nn.sigmoid(glu_b)
    c = causal_depthwise_conv(c, conv_dw_w) + conv_dw_b
    c = jax.nn.silu(layer_norm(c, conv_ln_g, conv_ln_b))
    y_conv = c @ w_conv_proj + b_conv_proj

    qkv = jax.nn.silu(causal_depthwise_conv(qkv, short_conv_w))
    q, k, v = jnp.split(qkv, [KEY_DIM, 2 * KEY_DIM], axis=-1)
    q = l2_normalize(q.reshape(bsz, seq, N_HEADS, HEAD_DIM_K).astype(jnp.float32))
    k = l2_normalize(k.reshape(bsz, seq, N_HEADS, HEAD_DIM_K).astype(jnp.float32))
    v = v.reshape(bsz, seq, N_HEADS, HEAD_DIM_V).astype(jnp.float32)
    beta = jax.nn.sigmoid(beta_raw.astype(jnp.float32))
    log_a = -jnp.exp(a_log.astype(jnp.float32)) * jax.nn.softplus(
        a_raw.astype(jnp.float32) + dt_bias.astype(jnp.float32))
    o = gated_delta_rule(q, k, v, beta, log_a)
    o = o * lax.rsqrt(jnp.mean(jnp.square(o), axis=-1, keepdims=True) + 1e-6)
    o = o * gdn_norm_w.astype(jnp.float32) * jax.nn.silu(
        zgate.reshape(bsz, seq, N_HEADS, HEAD_DIM_V).astype(jnp.float32))
    y_gdn = o.reshape(bsz, seq, VALUE_DIM).astype(h.dtype) @ w_gdn_proj

    m = jax.nn.sigmoid(gate_a) * y_conv + jax.nn.sigmoid(gate_b) * y_gdn
    return m @ w_out


def swiglu(h, w_ffn_in, w_ffn_out):
    gate, up = jnp.split(h @ w_ffn_in, 2, axis=-1)
    return (jax.nn.silu(gate) * up) @ w_ffn_out


def setup_inputs(seed: int = 0) -> dict:
    key = jax.random.key(seed)
    ks = jax.random.split(key, 24)
    f32 = jnp.float32
    L = DEPTH

    def nrm(k, shape, scale):
        return jax.random.normal(k, shape, f32) * scale

    dt = jnp.exp(jax.random.uniform(ks[10], (L, N_HEADS), f32, np.log(1e-3), np.log(1e-1)))
    return {
        "x": jax.random.normal(ks[0], (BATCH, SEQ, D_MODEL), f32),
        "w_in": nrm(ks[1], (L, D_MODEL, IN_WIDTH), D_MODEL ** -0.5),
        "b_in": nrm(ks[2], (L, IN_WIDTH), 0.02),
        "conv_dw_w": nrm(ks[3], (L, CONV_WIDTH, CONV_CHANNELS), CONV_WIDTH ** -0.5),
        "conv_dw_b": nrm(ks[4], (L, CONV_CHANNELS), 0.02),
        "conv_ln_g": 1.0 + nrm(ks[5], (L, CONV_CHANNELS), 0.02),
        "conv_ln_b": nrm(ks[6], (L, CONV_CHANNELS), 0.02),
        "w_conv_proj": nrm(ks[7], (L, CONV_CHANNELS, D_MODEL), CONV_CHANNELS ** -0.5),
        "b_conv_proj": nrm(ks[8], (L, D_MODEL), 0.02),
        "short_conv_w": nrm(ks[9], (L, SHORT_CONV, 2 * KEY_DIM + VALUE_DIM), SHORT_CONV ** -0.5),
        "a_log": jnp.log(jax.random.uniform(ks[11], (L, N_HEADS), f32, 1.0, 16.0)),
        "dt_bias": dt + jnp.log(-jnp.expm1(-dt)),
        "gdn_norm_w": 1.0 + nrm(ks[12], (L, HEAD_DIM_V), 0.02),
        "w_gdn_proj": nrm(ks[13], (L, VALUE_DIM, D_MODEL), VALUE_DIM ** -0.5),
        "w_out": nrm(ks[14], (L, D_MODEL, D_MODEL), DEEPNORM_BETA * D_MODEL ** -0.5),
        "ln1_g": 1.0 + nrm(ks[15], (L, D_MODEL), 0.02),
        "ln1_b": nrm(ks[16], (L, D_MODEL), 0.02),
        "w_ffn_in": nrm(ks[17], (L, D_MODEL, 2 * D_FF), D_MODEL ** -0.5),
        "w_ffn_out": nrm(ks[18], (L, D_FF, D_MODEL), DEEPNORM_BETA * D_FF ** -0.5),
        "ln2_g": 1.0 + nrm(ks[19], (L, D_MODEL), 0.02),
        "ln2_b": nrm(ks[20], (L, D_MODEL), 0.02),
    }


def reference(x, w_in, b_in, conv_dw_w, conv_dw_b, conv_ln_g, conv_ln_b, w_conv_proj, b_conv_proj,
              short_conv_w, a_log, dt_bias, gdn_norm_w, w_gdn_proj, w_out, ln1_g, ln1_b,
              w_ffn_in, w_ffn_out, ln2_g, ln2_b):
    for l in range(DEPTH):
        mix = mixer(x, w_in[l], b_in[l], conv_dw_w[l], conv_dw_b[l], conv_ln_g[l], conv_ln_b[l],
                    w_conv_proj[l], b_conv_proj[l], short_conv_w[l], a_log[l], dt_bias[l],
                    gdn_norm_w[l], w_gdn_proj[l], w_out[l])
        x = layer_norm(DEEPNORM_ALPHA * x + mix, ln1_g[l], ln1_b[l])
        x = layer_norm(DEEPNORM_ALPHA * x + swiglu(x, w_ffn_in[l], w_ffn_out[l]), ln2_g[l], ln2_b[l])
    return x
```

```python
import functools

import jax
import jax.numpy as jnp
from jax import lax
from jax.experimental import pallas as pl
from jax.experimental.pallas import tpu as pltpu

F32 = jnp.float32
BF16 = jnp.bfloat16
HIGHEST = lax.Precision.HIGHEST

HEAD_DIM = 128
CONV_WIDTH = 31
SHORT_CONV = 4
CHUNK = 64
INV_BLOCK = 16
LN_EPS = 1e-5
LANES = 128
SUBLANES = 8
VMEM_LIMIT = 48 * 1024 * 1024


def _cparams(*sem):
    return pltpu.CompilerParams(dimension_semantics=sem, vmem_limit_bytes=VMEM_LIMIT)


def _dot(a, b):
    return jnp.dot(a, b, preferred_element_type=F32)


def _dot_hi(a, b):
    return jnp.dot(a, b, preferred_element_type=F32, precision=HIGHEST)


def _dot_nt_hi(a, b):
    return lax.dot_general(a, b, (((1,), (1,)), ((), ())), precision=HIGHEST,
                           preferred_element_type=F32)


def _dot_tn_hi(a, b):
    return lax.dot_general(a, b, (((0,), (0,)), ((), ())), precision=HIGHEST,
                           preferred_element_type=F32)


def _sigmoid(x):
    return jax.nn.sigmoid(x)


def _silu(x):
    return x * jax.nn.sigmoid(x)


def _layer_norm(y, g, b):
    mu = jnp.mean(y, axis=-1, keepdims=True)
    d = y - mu
    var = jnp.mean(d * d, axis=-1, keepdims=True)
    return d * lax.rsqrt(var + LN_EPS) * g + b


def _glu_kernel(x_ref, wa_ref, wb_ref, ba_ref, bb_ref, o_ref):
    x = x_ref[...]
    a = _dot(x, wa_ref[...]) + ba_ref[...]
    b = _dot(x, wb_ref[...]) + bb_ref[...]
    o_ref[...] = a * _sigmoid(b)


def _glu_proj(xb, wa, wb, ba, bb, *, tm, tn):
    m, k = xb.shape
    n = wa.shape[1]
    return pl.pallas_call(
        _glu_kernel,
        grid=(n // tn, m // tm),
        in_specs=[
            pl.BlockSpec((tm, k), lambda j, i: (i, 0)),
            pl.BlockSpec((k, tn), lambda j, i: (0, j)),
            pl.BlockSpec((k, tn), lambda j, i: (0, j)),
            pl.BlockSpec((1, tn), lambda j, i: (0, j)),
            pl.BlockSpec((1, tn), lambda j, i: (0, j)),
        ],
        out_specs=pl.BlockSpec((tm, tn), lambda j, i: (i, j)),
        out_shape=jax.ShapeDtypeStruct((m, n), F32),
        compiler_params=_cparams("parallel", "arbitrary"),
        name="glu_proj",
    )(xb, wa, wb, ba, bb)


def _linear_kernel(x_ref, w_ref, b_ref, o_ref):
    o_ref[...] = _dot(x_ref[...], w_ref[...]) + b_ref[...]


def _linear(xb, w, b, *, tm, tn):
    m, k = xb.shape
    n = w.shape[1]
    return pl.pallas_call(
        _linear_kernel,
        grid=(n // tn, m // tm),
        in_specs=[
            pl.BlockSpec((tm, k), lambda j, i: (i, 0)),
            pl.BlockSpec((k, tn), lambda j, i: (0, j)),
            pl.BlockSpec((1, tn), lambda j, i: (0, j)),
        ],
        out_specs=pl.BlockSpec((tm, tn), lambda j, i: (i, j)),
        out_shape=jax.ShapeDtypeStruct((m, n), F32),
        compiler_params=_cparams("parallel", "arbitrary"),
        name="qkv_proj",
    )(xb, w, b)


def _gates_kernel(x_ref, w_ref, b_ref, o_ref, *, n_silu):
    acc = _dot(x_ref[...], w_ref[...]) + b_ref[...]
    j = pl.program_id(0)

    @pl.when(j < n_silu)
    def _():
        o_ref[...] = _silu(acc)

    @pl.when(j >= n_silu)
    def _():
        o_ref[...] = _sigmoid(acc)


def _gates_proj(xb, w, b, *, n_silu_cols, tm, tn):
    m, k = xb.shape
    n = w.shape[1]
    return pl.pallas_call(
        functools.partial(_gates_kernel, n_silu=n_silu_cols // tn),
        grid=(n // tn, m // tm),
        in_specs=[
            pl.BlockSpec((tm, k), lambda j, i: (i, 0)),
            pl.BlockSpec((k, tn), lambda j, i: (0, j)),
            pl.BlockSpec((1, tn), lambda j, i: (0, j)),
        ],
        out_specs=pl.BlockSpec((tm, tn), lambda j, i: (i, j)),
        out_shape=jax.ShapeDtypeStruct((m, n), F32),
        compiler_params=_cparams("parallel", "arbitrary"),
        name="gates_proj",
    )(xb, w, b)


def _beta_decay_kernel(x_ref, w_ref, b_ref, a_ref, dt_ref, o_ref, *, n_heads):
    acc = _dot(x_ref[...], w_ref[...]) + b_ref[...]
    lane = lax.broadcasted_iota(jnp.int32, acc.shape, 1)
    beta = _sigmoid(acc)
    log_a = -jnp.exp(a_ref[...]) * jax.nn.softplus(acc + dt_ref[...])
    o_ref[...] = jnp.where(lane < n_heads, beta, log_a)


def _beta_decay_proj(xb, w, b, a_log_row, dt_row, *, n_heads, tm):
    m, k = xb.shape
    return pl.pallas_call(
        functools.partial(_beta_decay_kernel, n_heads=n_heads),
        grid=(m // tm,),
        in_specs=[
            pl.BlockSpec((tm, k), lambda i: (i, 0)),
            pl.BlockSpec((k, LANES), lambda i: (0, 0)),
            pl.BlockSpec((1, LANES), lambda i: (0, 0)),
            pl.BlockSpec((1, LANES), lambda i: (0, 0)),
            pl.BlockSpec((1, LANES), lambda i: (0, 0)),
        ],
        out_specs=pl.BlockSpec((tm, LANES), lambda i: (i, 0)),
        out_shape=jax.ShapeDtypeStruct((m, LANES), F32),
        compiler_params=_cparams("parallel"),
        name="beta_decay_proj",
    )(xb, w, b, a_log_row, dt_row)


def _conv_ln_kernel(halo_ref, x_ref, w_ref, b_ref, g_ref, bn_ref, o_ref, xs_ref, *, ts, halo, rb):
    i = pl.program_id(0)

    @pl.when(i == 0)
    def _():
        xs_ref[0:halo, :] = jnp.zeros((halo, xs_ref.shape[1]), F32)

    @pl.when(i > 0)
    def _():
        xs_ref[0:halo, :] = halo_ref[...]

    xs_ref[halo:halo + ts, :] = x_ref[...]
    lead = halo - (CONV_WIDTH - 1)

    def row_block(r, carry):
        r0 = pl.multiple_of(r * rb, rb)
        win = xs_ref[pl.ds(r0, rb + halo), :]
        acc = jnp.zeros((rb, win.shape[1]), F32) + b_ref[...]
        for j in range(CONV_WIDTH):
            acc = acc + win[lead + j:lead + j + rb, :] * w_ref[j:j + 1, :]
        y = _layer_norm(acc, g_ref[...], bn_ref[...])
        o_ref[pl.ds(r0, rb), :] = _silu(y).astype(o_ref.dtype)
        return carry

    lax.fori_loop(0, ts // rb, row_block, 0)


def _conv_ln(c, w, b, g, bn, *, ts):
    s, ch = c.shape
    halo = 32
    rb = 8
    return pl.pallas_call(
        functools.partial(_conv_ln_kernel, ts=ts, halo=halo, rb=rb),
        grid=(s // ts,),
        in_specs=[
            pl.BlockSpec((halo, ch), lambda i: (jnp.maximum(i * (ts // halo) - 1, 0), 0)),
            pl.BlockSpec((ts, ch), lambda i: (i, 0)),
            pl.BlockSpec((CONV_WIDTH, ch), lambda i: (0, 0)),
            pl.BlockSpec((1, ch), lambda i: (0, 0)),
            pl.BlockSpec((1, ch), lambda i: (0, 0)),
            pl.BlockSpec((1, ch), lambda i: (0, 0)),
        ],
        out_specs=pl.BlockSpec((ts, ch), lambda i: (i, 0)),
        out_shape=jax.ShapeDtypeStruct((s, ch), BF16),
        scratch_shapes=[pltpu.VMEM((halo + ts, ch), F32)],
        compiler_params=_cparams("parallel"),
        name="conv_ln",
    )(c, c, w, b, g, bn)


def _qkv_prep_kernel(halo_ref, x_ref, w_ref, o_ref, xs_ref, *, ts, halo, n_norm, q_scale):
    i = pl.program_id(1)
    j = pl.program_id(0)

    @pl.when(i == 0)
    def _():
        xs_ref[0:halo, :] = jnp.zeros((halo, xs_ref.shape[1]), F32)

    @pl.when(i > 0)
    def _():
        xs_ref[0:halo, :] = halo_ref[...]

    xs_ref[halo:halo + ts, :] = x_ref[...]
    lead = halo - (SHORT_CONV - 1)
    acc = xs_ref[lead:lead + ts, :] * w_ref[0:1, :]
    for t in range(1, SHORT_CONV):
        acc = acc + xs_ref[lead + t:lead + t + ts, :] * w_ref[t:t + 1, :]
    y = _silu(acc)
    n_heads_blk = y.shape[1] // HEAD_DIM
    is_qk = j < 2 * n_norm
    scale = jnp.where(j < n_norm, q_scale, 1.0).astype(F32)
    for h in range(n_heads_blk):
        yh = y[:, h * HEAD_DIM:(h + 1) * HEAD_DIM]
        nrm = yh * lax.rsqrt(jnp.sum(yh * yh, axis=-1, keepdims=True) + 1e-6) * scale
        o_ref[:, h * HEAD_DIM:(h + 1) * HEAD_DIM] = jnp.where(is_qk, nrm, yh)


def _qkv_prep(qkv, w, *, key_dim, ts, tc):
    s, n = qkv.shape
    halo = SUBLANES
    return pl.pallas_call(
        functools.partial(_qkv_prep_kernel, ts=ts, halo=halo, n_norm=key_dim // tc,
                          q_scale=HEAD_DIM ** -0.5),
        grid=(n // tc, s // ts),
        in_specs=[
            pl.BlockSpec((halo, tc), lambda j, i: (jnp.maximum(i * (ts // halo) - 1, 0), j)),
            pl.BlockSpec((ts, tc), lambda j, i: (i, j)),
            pl.BlockSpec((SHORT_CONV, tc), lambda j, i: (0, j)),
        ],
        out_specs=pl.BlockSpec((ts, tc), lambda j, i: (i, j)),
        out_shape=jax.ShapeDtypeStruct((s, n), F32),
        scratch_shapes=[pltpu.VMEM((halo + ts, tc), F32)],
        compiler_params=_cparams("parallel", "parallel"),
        name="qkv_prep",
    )(qkv, qkv, w)


def _inv_unit_lower(low, eye, same_block):
    d = jnp.where(same_block, low, 0.0)
    n = low - d
    p = eye - d
    pw = d
    for _ in range(3):
        pw = _dot_hi(pw, pw)
        p = p + _dot_hi(p, pw)
    m = _dot_hi(p, n)
    m2 = _dot_hi(m, m)
    q = eye - m + m2 - _dot_hi(m, m2)
    return _dot_hi(q, p)


def _gdn_kernel(q_ref, k_ref, v_ref, bd_ref, sz_ref, nw_ref, o_ref, s_ref, *, hg, ts, n_heads):
    c = CHUNK
    hgrp = pl.program_id(0)

    @pl.when(pl.program_id(1) == 0)
    def _():
        s_ref[...] = jnp.zeros(s_ref.shape, F32)

    ri = lax.broadcasted_iota(jnp.int32, (c, c), 0)
    ci = lax.broadcasted_iota(jnp.int32, (c, c), 1)
    causal = ri >= ci
    strict = ri > ci
    tri = causal.astype(F32)
    eye = (ri == ci).astype(F32)
    same_block = (ri // INV_BLOCK) == (ci // INV_BLOCK)
    sel_r = lax.broadcasted_iota(jnp.int32, (LANES, LANES), 0)
    sel_c = lax.broadcasted_iota(jnp.int32, (c, LANES), 1)
    nw = nw_ref[...]

    def chunk_body(n, carry):
        r0 = pl.multiple_of(n * c, c)
        bd = bd_ref[pl.ds(r0, c), :]
        gcum = _dot_hi(tri, bd)
        for hl in range(hg):
            head = hgrp * hg + hl
            cols = slice(hl * HEAD_DIM, (hl + 1) * HEAD_DIM)
            beta_b = _dot_hi(bd, (sel_r == head).astype(F32))
            g_b = _dot_hi(gcum, (sel_r == n_heads + head).astype(F32))
            g_row = _dot_nt_hi((sel_c == n_heads + head).astype(F32), gcum)
            diff = jnp.where(causal, g_b[:, :c] - g_row, 0.0)
            decay = jnp.where(causal, jnp.exp(diff), 0.0)
            q = q_ref[pl.ds(r0, c), cols]
            k = k_ref[pl.ds(r0, c), cols]
            v = v_ref[pl.ds(r0, c), cols]
            kb = k * beta_b
            low = jnp.where(strict, _dot_nt_hi(kb, k) * decay, 0.0)
            t_inv = _inv_unit_lower(low, eye, same_block)
            eg = jnp.exp(g_b)
            u = _dot_hi(t_inv, v * beta_b)
            w = _dot_hi(t_inv, kb * eg)
            attn = _dot_nt_hi(q, k) * decay
            g_last = g_b[c - 1:c, :]
            kd = k * jnp.exp(g_last - g_b)
            state = s_ref[hl]
            v_new = u - _dot_hi(w, state)
            o = _dot_hi(q * eg, state) + _dot_hi(attn, v_new)
            s_ref[hl] = state * jnp.exp(g_last) + _dot_tn_hi(kd, v_new)
            o = o * lax.rsqrt(jnp.mean(o * o, axis=-1, keepdims=True) + 1e-6)
            o = o * nw * sz_ref[pl.ds(r0, c), cols]
            o_ref[pl.ds(r0, c), cols] = o.astype(o_ref.dtype)
        return carry

    lax.fori_loop(0, ts // c, chunk_body, 0)


def _gdn(qkvn, bd, gates, nw, *, n_heads, hg, ts):
    s = qkvn.shape[0]
    width = hg * HEAD_DIM
    nblk = n_heads // hg
    return pl.pallas_call(
        functools.partial(_gdn_kernel, hg=hg, ts=ts, n_heads=n_heads),
        grid=(nblk, s // ts),
        in_specs=[
            pl.BlockSpec((ts, width), lambda h, t: (t, h)),
            pl.BlockSpec((ts, width), lambda h, t: (t, nblk + h)),
            pl.BlockSpec((ts, width), lambda h, t: (t, 2 * nblk + h)),
            pl.BlockSpec((ts, LANES), lambda h, t: (t, 0)),
            pl.BlockSpec((ts, width), lambda h, t: (t, h)),
            pl.BlockSpec((1, HEAD_DIM), lambda h, t: (0, 0)),
        ],
        out_specs=pl.BlockSpec((ts, width), lambda h, t: (t, h)),
        out_shape=jax.ShapeDtypeStruct((s, n_heads * HEAD_DIM), BF16),
        scratch_shapes=[pltpu.VMEM((hg, HEAD_DIM, HEAD_DIM), F32)],
        compiler_params=_cparams("parallel", "arbitrary"),
        name="gated_delta_rule",
    )(qkvn, qkvn, qkvn, bd, gates, nw)


def _merge_kernel(c_ref, o_ref_in, wc_ref, wg_ref, bc_ref, ga_ref, gb_ref, m_ref):
    yc = _dot(c_ref[...], wc_ref[...]) + bc_ref[...]
    yg = _dot(o_ref_in[...], wg_ref[...])
    m_ref[...] = (ga_ref[...] * yc + gb_ref[...] * yg).astype(m_ref.dtype)


def _merge_proj(c2, og, wc, wg, bc, gates, *, gate_col0, tm, tn):
    m, k = c2.shape
    n = wc.shape[1]
    ga0 = gate_col0 // tn
    gb0 = (gate_col0 + n) // tn
    return pl.pallas_call(
        _merge_kernel,
        grid=(n // tn, m // tm),
        in_specs=[
            pl.BlockSpec((tm, k), lambda j, i: (i, 0)),
            pl.BlockSpec((tm, k), lambda j, i: (i, 0)),
            pl.BlockSpec((k, tn), lambda j, i: (0, j)),
            pl.BlockSpec((k, tn), lambda j, i: (0, j)),
            pl.BlockSpec((1, tn), lambda j, i: (0, j)),
            pl.BlockSpec((tm, tn), lambda j, i: (i, ga0 + j)),
            pl.BlockSpec((tm, tn), lambda j, i: (i, gb0 + j)),
        ],
        out_specs=pl.BlockSpec((tm, tn), lambda j, i: (i, j)),
        out_shape=jax.ShapeDtypeStruct((m, n), BF16),
        compiler_params=_cparams("parallel", "arbitrary"),
        name="merge_proj",
    )(c2, og, wc, wg, bc, gates, gates)


def _proj_ln_kernel(a_ref, w_ref, res_ref, g_ref, b_ref, o_ref, ob_ref, acc_ref, *, alpha):
    kk = pl.program_id(1)

    @pl.when(kk == 0)
    def _():
        acc_ref[...] = alpha * res_ref[...]

    acc_ref[...] += _dot(a_ref[...], w_ref[...])

    @pl.when(kk == pl.num_programs(1) - 1)
    def _():
        y = _layer_norm(acc_ref[...], g_ref[...], b_ref[...])
        o_ref[...] = y
        ob_ref[...] = y.astype(ob_ref.dtype)


def _proj_ln(a, w, res, g, b, *, alpha, tm, tk):
    m, k = a.shape
    n = w.shape[1]
    return pl.pallas_call(
        functools.partial(_proj_ln_kernel, alpha=alpha),
        grid=(m // tm, k // tk),
        in_specs=[
            pl.BlockSpec((tm, tk), lambda i, kk: (i, kk)),
            pl.BlockSpec((tk, n), lambda i, kk: (kk, 0)),
            pl.BlockSpec((tm, n), lambda i, kk: (i, 0)),
            pl.BlockSpec((1, n), lambda i, kk: (0, 0)),
            pl.BlockSpec((1, n), lambda i, kk: (0, 0)),
        ],
        out_specs=[
            pl.BlockSpec((tm, n), lambda i, kk: (i, 0)),
            pl.BlockSpec((tm, n), lambda i, kk: (i, 0)),
        ],
        out_shape=[jax.ShapeDtypeStruct((m, n), F32), jax.ShapeDtypeStruct((m, n), BF16)],
        scratch_shapes=[pltpu.VMEM((tm, n), F32)],
        compiler_params=_cparams("parallel", "arbitrary"),
        name="proj_ln",
    )(a, w, res, g, b)


def _swiglu_in_kernel(x_ref, wg_ref, wu_ref, o_ref):
    x = x_ref[...]
    gate = _dot(x, wg_ref[...])
    up = _dot(x, wu_ref[...])
    o_ref[...] = (_silu(gate) * up).astype(o_ref.dtype)


def _swiglu_in(xb, w, *, tm, tn):
    m, k = xb.shape
    d_ff = w.shape[1] // 2
    up0 = d_ff // tn
    return pl.pallas_call(
        _swiglu_in_kernel,
        grid=(d_ff // tn, m // tm),
        in_specs=[
            pl.BlockSpec((tm, k), lambda j, i: (i, 0)),
            pl.BlockSpec((k, tn), lambda j, i: (0, j)),
            pl.BlockSpec((k, tn), lambda j, i: (0, up0 + j)),
        ],
        out_specs=pl.BlockSpec((tm, tn), lambda j, i: (i, j)),
        out_shape=jax.ShapeDtypeStruct((m, d_ff), BF16),
        compiler_params=_cparams("parallel", "arbitrary"),
        name="swiglu_in",
    )(xb, w, w)


def _row(v):
    return v.reshape(1, -1).astype(F32)


def _pad_lanes(v, offset):
    return jnp.zeros((1, LANES), F32).at[0, offset:offset + v.shape[0]].set(v.astype(F32))


def _layer(x, xb, p, *, alpha):
    d_model = x.shape[1]
    n_heads = p["a_log"].shape[0]
    key_dim = n_heads * HEAD_DIM
    w_in, b_in = p["w_in"], p["b_in"]
    o_glu_b = d_model
    o_qkv = 2 * d_model
    o_z = o_qkv + 3 * key_dim
    o_beta = o_z + key_dim
    o_gate = o_beta + 2 * n_heads

    def wcols(lo, hi):
        return w_in[:, lo:hi].astype(BF16)

    c = _glu_proj(xb, wcols(0, o_glu_b), wcols(o_glu_b, o_qkv),
                  _row(b_in[0:o_glu_b]), _row(b_in[o_glu_b:o_qkv]), tm=512, tn=512)
    c2 = _conv_ln(c, p["conv_dw_w"], _row(p["conv_dw_b"]), _row(p["conv_ln_g"]),
                  _row(p["conv_ln_b"]), ts=256)

    qkv = _linear(xb, wcols(o_qkv, o_z), _row(b_in[o_qkv:o_z]), tm=512, tn=1024)
    qkvn = _qkv_prep(qkv, p["short_conv_w"], key_dim=key_dim, ts=512, tc=512)
    w_gates = jnp.concatenate([wcols(o_z, o_beta), wcols(o_gate, o_gate + 2 * d_model)], axis=1)
    b_gates = jnp.concatenate([b_in[o_z:o_beta], b_in[o_gate:o_gate + 2 * d_model]])
    gates = _gates_proj(xb, w_gates, _row(b_gates), n_silu_cols=key_dim, tm=512, tn=1024)
    w_bd = jnp.zeros((d_model, LANES), BF16).at[:, :2 * n_heads].set(wcols(o_beta, o_gate))
    bd = _beta_decay_proj(xb, w_bd, _pad_lanes(b_in[o_beta:o_gate], 0),
                          _pad_lanes(p["a_log"], n_heads), _pad_lanes(p["dt_bias"], n_heads),
                          n_heads=n_heads, tm=512)
    og = _gdn(qkvn, bd, gates, _row(p["gdn_norm_w"]), n_heads=n_heads, hg=2, ts=512)

    m = _merge_proj(c2, og, p["w_conv_proj"].astype(BF16), p["w_gdn_proj"].astype(BF16),
                    _row(p["b_conv_proj"]), gates, gate_col0=key_dim, tm=512, tn=512)
    x1, x1b = _proj_ln(m, p["w_out"].astype(BF16), x, _row(p["ln1_g"]), _row(p["ln1_b"]),
                       alpha=alpha, tm=512, tk=d_model)
    hff = _swiglu_in(x1b, p["w_ffn_in"].astype(BF16), tm=512, tn=512)
    d_ff = hff.shape[1]
    x2, x2b = _proj_ln(hff, p["w_ffn_out"].astype(BF16), x1, _row(p["ln2_g"]), _row(p["ln2_b"]),
                       alpha=alpha, tm=512, tk=d_ff // 4)
    return x2, x2b


def kernel(x, w_in, b_in, conv_dw_w, conv_dw_b, conv_ln_g, conv_ln_b, w_conv_proj, b_conv_proj,
           short_conv_w, a_log, dt_bias, gdn_norm_w, w_gdn_proj, w_out, ln1_g, ln1_b,
           w_ffn_in, w_ffn_out, ln2_g, ln2_b):
    bsz, seq, d_model = x.shape
    depth = w_in.shape[0]
    alpha = (2.0 * depth) ** 0.25
    params = dict(w_in=w_in, b_in=b_in, conv_dw_w=conv_dw_w, conv_dw_b=conv_dw_b,
                  conv_ln_g=conv_ln_g, conv_ln_b=conv_ln_b, w_conv_proj=w_conv_proj,
                  b_conv_proj=b_conv_proj, short_conv_w=short_conv_w, a_log=a_log, dt_bias=dt_bias,
                  gdn_norm_w=gdn_norm_w, w_gdn_proj=w_gdn_proj, w_out=w_out, ln1_g=ln1_g,
                  ln1_b=ln1_b, w_ffn_in=w_ffn_in, w_ffn_out=w_ffn_out, ln2_g=ln2_g, ln2_b=ln2_b)
    outs = []
    for bi in range(bsz):
        h = x[bi]
        hb = h.astype(BF16)
        for l in range(depth):
            h, hb = _layer(h, hb, {k: v[l] for k, v in params.items()}, alpha=alpha)
        outs.append(h)
    return jnp.stack(outs, axis=0)
```

```python
import functools

import jax
import jax.numpy as jnp
from jax import lax
from jax.experimental import pallas as pl
from jax.experimental.pallas import tpu as pltpu

F32 = jnp.float32
BF16 = jnp.bfloat16
HIGHEST = lax.Precision.HIGHEST

HEAD_DIM = 128
CONV_WIDTH = 31
SHORT_CONV = 4
CHUNK = 64
INV_BLOCK = 16
LN_EPS = 1e-5
LANES = 128
SUBLANES = 8
VMEM_LIMIT = 48 * 1024 * 1024


def _cparams(*sem):
    return pltpu.CompilerParams(dimension_semantics=sem, vmem_limit_bytes=VMEM_LIMIT)


def _dot(a, b):
    return jnp.dot(a, b, preferred_element_type=F32)


def _dot_hi(a, b):
    return jnp.dot(a, b, preferred_element_type=F32, precision=HIGHEST)


def _dot_nt(a, b):
    return lax.dot_general(a, b, (((1,), (1,)), ((), ())), preferred_element_type=F32)


def _dot_tn(a, b):
    return lax.dot_general(a, b, (((0,), (0,)), ((), ())), preferred_element_type=F32)


def _bf(x):
    return x.astype(BF16)


def _sigmoid(x):
    return 0.5 * jnp.tanh(0.5 * x) + 0.5


def _silu(x):
    hx = 0.5 * x
    return hx * jnp.tanh(hx) + hx


def _layer_norm(y, g, b):
    mu = jnp.mean(y, axis=-1, keepdims=True)
    d = y - mu
    var = jnp.mean(d * d, axis=-1, keepdims=True)
    return d * lax.rsqrt(var + LN_EPS) * g + b


def _glu_kernel(x_ref, wa_ref, wb_ref, ba_ref, bb_ref, o_ref):
    x = x_ref[...]
    a = _dot(x, wa_ref[...]) + ba_ref[...]
    b = _dot(x, wb_ref[...]) + bb_ref[...]
    o_ref[...] = a * _sigmoid(b)


def _glu_proj(xb, wa, wb, ba, bb, *, tm, tn):
    m, k = xb.shape
    n = wa.shape[1]
    return pl.pallas_call(
        _glu_kernel,
        grid=(n // tn, m // tm),
        in_specs=[
            pl.BlockSpec((tm, k), lambda j, i: (i, 0)),
            pl.BlockSpec((k, tn), lambda j, i: (0, j)),
            pl.BlockSpec((k, tn), lambda j, i: (0, j)),
            pl.BlockSpec((1, tn), lambda j, i: (0, j)),
            pl.BlockSpec((1, tn), lambda j, i: (0, j)),
        ],
        out_specs=pl.BlockSpec((tm, tn), lambda j, i: (i, j)),
        out_shape=jax.ShapeDtypeStruct((m, n), F32),
        compiler_params=_cparams("parallel", "arbitrary"),
        name="glu_proj",
    )(xb, wa, wb, ba, bb)


def _linear_kernel(x_ref, w_ref, b_ref, o_ref):
    o_ref[...] = (_dot(x_ref[...], w_ref[...]) + b_ref[...]).astype(o_ref.dtype)


def _linear(xb, w, b, *, tm, tn):
    m, k = xb.shape
    n = w.shape[1]
    return pl.pallas_call(
        _linear_kernel,
        grid=(n // tn, m // tm),
        in_specs=[
            pl.BlockSpec((tm, k), lambda j, i: (i, 0)),
            pl.BlockSpec((k, tn), lambda j, i: (0, j)),
            pl.BlockSpec((1, tn), lambda j, i: (0, j)),
        ],
        out_specs=pl.BlockSpec((tm, tn), lambda j, i: (i, j)),
        out_shape=jax.ShapeDtypeStruct((m, n), BF16),
        compiler_params=_cparams("parallel", "arbitrary"),
        name="qkv_proj",
    )(xb, w, b)


def _gates_kernel(x_ref, w_ref, b_ref, o_ref, *, n_silu):
    acc = _dot(x_ref[...], w_ref[...]) + b_ref[...]
    j = pl.program_id(0)

    @pl.when(j < n_silu)
    def _():
        o_ref[...] = _silu(acc).astype(o_ref.dtype)

    @pl.when(j >= n_silu)
    def _():
        o_ref[...] = _sigmoid(acc).astype(o_ref.dtype)


def _gates_proj(xb, w, b, *, n_silu_cols, tm, tn):
    m, k = xb.shape
    n = w.shape[1]
    return pl.pallas_call(
        functools.partial(_gates_kernel, n_silu=n_silu_cols // tn),
        grid=(n // tn, m // tm),
        in_specs=[
            pl.BlockSpec((tm, k), lambda j, i: (i, 0)),
            pl.BlockSpec((k, tn), lambda j, i: (0, j)),
            pl.BlockSpec((1, tn), lambda j, i: (0, j)),
        ],
        out_specs=pl.BlockSpec((tm, tn), lambda j, i: (i, j)),
        out_shape=jax.ShapeDtypeStruct((m, n), BF16),
        compiler_params=_cparams("parallel", "arbitrary"),
        name="gates_proj",
    )(xb, w, b)


def _beta_decay_kernel(x_ref, w_ref, b_ref, a_ref, dt_ref, o_ref, ot_ref, *, n_heads):
    acc = _dot(x_ref[...], w_ref[...]) + b_ref[...]
    lane = lax.broadcasted_iota(jnp.int32, acc.shape, 1)
    beta = _sigmoid(acc)
    g = -jnp.exp(a_ref[...]) * jax.nn.softplus(acc + dt_ref[...])
    pos = lax.broadcasted_iota(jnp.int32, acc.shape, 0) % CHUNK
    shift = 1
    while shift < CHUNK:
        g = g + jnp.where(pos >= shift, pltpu.roll(g, shift, 0), 0.0)
        shift *= 2
    out = jnp.where(lane < n_heads, beta, g)
    o_ref[...] = out
    ot_ref[...] = out.T


def _beta_decay_proj(xb, w, b, a_log_row, dt_row, *, n_heads, tm):
    m, k = xb.shape
    return pl.pallas_call(
        functools.partial(_beta_decay_kernel, n_heads=n_heads),
        grid=(m // tm,),
        in_specs=[
            pl.BlockSpec((tm, k), lambda i: (i, 0)),
            pl.BlockSpec((k, LANES), lambda i: (0, 0)),
            pl.BlockSpec((1, LANES), lambda i: (0, 0)),
            pl.BlockSpec((1, LANES), lambda i: (0, 0)),
            pl.BlockSpec((1, LANES), lambda i: (0, 0)),
        ],
        out_specs=[pl.BlockSpec((tm, LANES), lambda i: (i, 0)),
                   pl.BlockSpec((LANES, tm), lambda i: (0, i))],
        out_shape=[jax.ShapeDtypeStruct((m, LANES), F32), jax.ShapeDtypeStruct((LANES, m), F32)],
        compiler_params=_cparams("parallel"),
        name="beta_decay_proj",
    )(xb, w, b, a_log_row, dt_row)


def _conv_ln_kernel(halo_ref, x_ref, w_ref, b_ref, g_ref, bn_ref, o_ref, xs_ref, y_ref, *,
                    ts, halo, rb):
    i = pl.program_id(0)
    ch = xs_ref.shape[1]

    @pl.when(i == 0)
    def _():
        xs_ref[0:halo, :] = jnp.zeros((halo, ch), F32)

    @pl.when(i > 0)
    def _():
        xs_ref[0:halo, :] = halo_ref[...]

    xs_ref[halo:halo + ts, :] = x_ref[...]
    n_rb = ts // rb

    def conv_block(idx, carry):
        c0 = pl.multiple_of((idx // n_rb) * LANES, LANES)
        r0 = (idx % n_rb) * rb
        lanes = pl.ds(c0, LANES)
        acc = jnp.zeros((rb, LANES), F32) + b_ref[:, lanes]
        for r in range(SUBLANES):
            lead = 0 if r == 0 else SUBLANES
            z = None
            for p in range((CONV_WIDTH - 1 - r) // SUBLANES + 1):
                j = CONV_WIDTH - 1 - (SUBLANES * p + r)
                row = pl.multiple_of(r0 + (halo - lead - SUBLANES * p), SUBLANES)
                term = xs_ref[pl.ds(row, rb + lead), lanes] * w_ref[j:j + 1, lanes]
                z = term if z is None else z + term
            acc = acc + (z if r == 0 else z[SUBLANES - r:SUBLANES - r + rb])
        y_ref[pl.ds(pl.multiple_of(r0, rb), rb), lanes] = acc
        return carry

    lax.fori_loop(0, n_rb * (ch // LANES), conv_block, 0)

    ln_rows = 2 * SUBLANES

    def ln_block(r, carry):
        r0 = pl.multiple_of(r * ln_rows, ln_rows)
        y = _layer_norm(y_ref[pl.ds(r0, ln_rows), :], g_ref[...], bn_ref[...])
        o_ref[pl.ds(r0, ln_rows), :] = _silu(y).astype(o_ref.dtype)
        return carry

    lax.fori_loop(0, ts // ln_rows, ln_block, 0)


def _conv_ln(c, w, b, g, bn, *, ts):
    s, ch = c.shape
    halo = 4 * SUBLANES
    rb = 8 * SUBLANES
    return pl.pallas_call(
        functools.partial(_conv_ln_kernel, ts=ts, halo=halo, rb=rb),
        grid=(s // ts,),
        in_specs=[
            pl.BlockSpec((halo, ch), lambda i: (jnp.maximum(i * (ts // halo) - 1, 0), 0)),
            pl.BlockSpec((ts, ch), lambda i: (i, 0)),
            pl.BlockSpec((CONV_WIDTH, ch), lambda i: (0, 0)),
            pl.BlockSpec((1, ch), lambda i: (0, 0)),
            pl.BlockSpec((1, ch), lambda i: (0, 0)),
            pl.BlockSpec((1, ch), lambda i: (0, 0)),
        ],
        out_specs=pl.BlockSpec((ts, ch), lambda i: (i, 0)),
        out_shape=jax.ShapeDtypeStruct((s, ch), BF16),
        scratch_shapes=[pltpu.VMEM((halo + ts, ch), F32), pltpu.VMEM((ts, ch), F32)],
        compiler_params=_cparams("parallel"),
        name="conv_ln",
    )(c, c, w, b, g, bn)


def _qkv_prep_kernel(halo_ref, x_ref, w_ref, o_ref, xs_ref, *, ts, halo, n_norm, q_scale):
    i = pl.program_id(1)
    j = pl.program_id(0)

    @pl.when(i == 0)
    def _():
        xs_ref[0:halo, :] = jnp.zeros((halo, xs_ref.shape[1]), F32)

    @pl.when(i > 0)
    def _():
        xs_ref[0:halo, :] = halo_ref[...].astype(F32)

    xs_ref[halo:halo + ts, :] = x_ref[...].astype(F32)
    lead = halo - (SHORT_CONV - 1)
    acc = xs_ref[lead:lead + ts, :] * w_ref[0:1, :]
    for t in range(1, SHORT_CONV):
        acc = acc + xs_ref[lead + t:lead + t + ts, :] * w_ref[t:t + 1, :]
    y = _silu(acc)
    n_heads_blk = y.shape[1] // HEAD_DIM
    is_qk = j < 2 * n_norm
    scale = jnp.where(j < n_norm, q_scale, 1.0).astype(F32)
    for h in range(n_heads_blk):
        yh = y[:, h * HEAD_DIM:(h + 1) * HEAD_DIM]
        nrm = yh * lax.rsqrt(jnp.sum(yh * yh, axis=-1, keepdims=True) + 1e-6) * scale
        o_ref[:, h * HEAD_DIM:(h + 1) * HEAD_DIM] = jnp.where(is_qk, nrm, yh).astype(o_ref.dtype)


def _qkv_prep(qkv, w, *, key_dim, ts, tc):
    s, n = qkv.shape
    halo = 2 * SUBLANES
    return pl.pallas_call(
        functools.partial(_qkv_prep_kernel, ts=ts, halo=halo, n_norm=key_dim // tc,
                          q_scale=HEAD_DIM ** -0.5),
        grid=(n // tc, s // ts),
        in_specs=[
            pl.BlockSpec((halo, tc), lambda j, i: (jnp.maximum(i * (ts // halo) - 1, 0), j)),
            pl.BlockSpec((ts, tc), lambda j, i: (i, j)),
            pl.BlockSpec((SHORT_CONV, tc), lambda j, i: (0, j)),
        ],
        out_specs=pl.BlockSpec((ts, tc), lambda j, i: (i, j)),
        out_shape=jax.ShapeDtypeStruct((s, n), BF16),
        scratch_shapes=[pltpu.VMEM((halo + ts, tc), F32)],
        compiler_params=_cparams("parallel", "parallel"),
        name="qkv_prep",
    )(qkv, qkv, w)


def _gdn_kernel(q_ref, k_ref, v_ref, bd_ref, gt_ref, sz_ref, nw_ref, o_ref, s_ref, *,
                hg, ts, unit, n_heads):
    c = CHUNK
    hgrp = pl.program_id(0)

    @pl.when(pl.program_id(1) == 0)
    def _():
        s_ref[...] = jnp.zeros(s_ref.shape, F32)

    bd = bd_ref[...]
    lane = lax.broadcasted_iota(jnp.int32, (ts, LANES), 1)
    ri = lax.broadcasted_iota(jnp.int32, (unit, unit), 0)
    ci = lax.broadcasted_iota(jnp.int32, (unit, unit), 1)
    same_chunk = (ri // c) == (ci // c)
    causal = same_chunk & (ri >= ci)
    strict = same_chunk & (ri > ci)
    inv_block = (ri // INV_BLOCK) == (ci // INV_BLOCK)
    eye = (ri == ci).astype(F32)
    row_chunk = lax.broadcasted_iota(jnp.int32, (unit, 1), 0) // c
    nw = nw_ref[...]
    n_units = ts // unit
    cpu = unit // c
    cat = jnp.concatenate

    st = []
    for hl in range(hg):
        head = hgrp * hg + hl
        cols = slice(hl * HEAD_DIM, (hl + 1) * HEAD_DIM)
        beta_col = jnp.sum(jnp.where(lane == head, bd, 0.0), axis=1, keepdims=True)
        g_col = jnp.sum(jnp.where(lane == n_heads + head, bd, 0.0), axis=1, keepdims=True)
        g_row = gt_ref[pl.ds(n_heads + head, 1), :]
        for un in range(n_units):
            rows = slice(un * unit, (un + 1) * unit)
            gc, gr, bc = g_col[rows], g_row[:, rows], beta_col[rows]
            decay = jnp.where(causal, jnp.exp(jnp.where(causal, gc - gr, 0.0)), 0.0)
            qbf = q_ref[rows, cols]
            kbf = k_ref[rows, cols]
            k = kbf.astype(F32)
            kb = k * bc
            eg = jnp.exp(gc)
            gram = _dot_nt(cat([_bf(kb), qbf], axis=0), kbf)
            low = jnp.where(strict, gram[:unit] * decay, 0.0)
            dg = jnp.where(inv_block, low, 0.0)
            g_last = [gr[:, cc * c + c - 1:cc * c + c] for cc in range(cpu)]
            g_last_col = g_last[0]
            for cc in range(1, cpu):
                g_last_col = jnp.where(row_chunk == cc, g_last[cc], g_last_col)
            kd = _bf(k * jnp.exp(g_last_col - gc))
            st.append(dict(
                hl=hl, row0=un * unit, attn=_bf(gram[unit:] * decay), dg=dg, n=_bf(low - dg),
                p=eye - dg, rhs=_bf(cat([v_ref[rows, cols].astype(F32) * bc, kb * eg], axis=1)),
                qd=qbf.astype(F32) * eg, gamma=[jnp.exp(gl) for gl in g_last],
                kd=cat([jnp.where(row_chunk == cc, kd, jnp.zeros_like(kd)) for cc in range(cpu)], axis=1)))

    for d in st:
        dgb = _bf(d["dg"])
        d["pw"] = _dot(dgb, dgb)
    for _ in range(2):
        for d in st:
            pwb = _bf(d["pw"])
            r = _dot(cat([pwb, _bf(d["p"])], axis=0), pwb)
            d["pw"], d["p"] = r[:unit], d["p"] + r[unit:]
    for d in st:
        d["p"] = d["p"] + _dot(_bf(d["p"]), _bf(d["pw"]))
    for d in st:
        r = _dot(_bf(d["p"]), cat([d["n"], d["rhs"]], axis=1))
        d["m"], d["y"] = _bf(r[:, :unit]), r[:, unit:]
    for d in st:
        r = _dot(d["m"], cat([d["m"], _bf(d["y"])], axis=1))
        d["m2"], d["z"] = _bf(r[:, :unit]), d["y"] - r[:, unit:]
    for d in st:
        d["uw"] = _bf(d["z"] + _dot(d["m2"], _bf(d["z"])))
    for d in st:
        au = _dot(d["attn"], d["uw"])
        d["o_loc"] = au[:, :HEAD_DIM]
        d["q_eff"] = d["qd"] - au[:, HEAD_DIM:]
        d["bm"] = _dot_tn(d["kd"], d["uw"])

    states = [s_ref[hl] for hl in range(hg)]
    for un in range(n_units):
        for cc in range(cpu):
            for d in st:
                if d["row0"] != un * unit:
                    continue
                hl = d["hl"]
                cols = slice(hl * HEAD_DIM, (hl + 1) * HEAD_DIM)
                bm = d["bm"][cc * HEAD_DIM:(cc + 1) * HEAD_DIM]
                mq = cat([_bf(bm[:, HEAD_DIM:]), _bf(d["q_eff"][cc * c:(cc + 1) * c])], axis=0)
                ms_qs = _dot(mq, _bf(states[hl]))
                o = ms_qs[HEAD_DIM:] + d["o_loc"][cc * c:(cc + 1) * c]
                states[hl] = d["gamma"][cc] * states[hl] - ms_qs[:HEAD_DIM] + bm[:, :HEAD_DIM]
                rows = slice(d["row0"] + cc * c, d["row0"] + (cc + 1) * c)
                o = o * lax.rsqrt(jnp.mean(o * o, axis=-1, keepdims=True) + 1e-6)
                o = o * nw * sz_ref[rows, cols].astype(F32)
                o_ref[rows, cols] = o.astype(o_ref.dtype)
    for hl in range(hg):
        s_ref[hl] = states[hl]


def _gdn(qkvn, bd, bdt, gates, nw, *, n_heads, hg, ts, unit):
    s = qkvn.shape[0]
    width = hg * HEAD_DIM
    nblk = n_heads // hg
    return pl.pallas_call(
        functools.partial(_gdn_kernel, hg=hg, ts=ts, unit=unit, n_heads=n_heads),
        grid=(nblk, s // ts),
        in_specs=[
            pl.BlockSpec((ts, width), lambda h, t: (t, h)),
            pl.BlockSpec((ts, width), lambda h, t: (t, nblk + h)),
            pl.BlockSpec((ts, width), lambda h, t: (t, 2 * nblk + h)),
            pl.BlockSpec((ts, LANES), lambda h, t: (t, 0)),
            pl.BlockSpec((LANES, ts), lambda h, t: (0, t)),
            pl.BlockSpec((ts, width), lambda h, t: (t, h)),
            pl.BlockSpec((1, HEAD_DIM), lambda h, t: (0, 0)),
        ],
        out_specs=pl.BlockSpec((ts, width), lambda h, t: (t, h)),
        out_shape=jax.ShapeDtypeStruct((s, n_heads * HEAD_DIM), BF16),
        scratch_shapes=[pltpu.VMEM((hg, HEAD_DIM, HEAD_DIM), F32)],
        compiler_params=_cparams("parallel", "arbitrary"),
        name="gated_delta_rule",
    )(qkvn, qkvn, qkvn, bd, bdt, gates, nw)


def _merge_kernel(c_ref, o_ref_in, wc_ref, wg_ref, bc_ref, ga_ref, gb_ref, m_ref):
    yc = _dot(c_ref[...], wc_ref[...]) + bc_ref[...]
    yg = _dot(o_ref_in[...], wg_ref[...])
    m_ref[...] = (ga_ref[...] * yc + gb_ref[...] * yg).astype(m_ref.dtype)


def _merge_proj(c2, og, wc, wg, bc, gates, *, gate_col0, tm, tn):
    m, k = c2.shape
    n = wc.shape[1]
    ga0 = gate_col0 // tn
    gb0 = (gate_col0 + n) // tn
    return pl.pallas_call(
        _merge_kernel,
        grid=(n // tn, m // tm),
        in_specs=[
            pl.BlockSpec((tm, k), lambda j, i: (i, 0)),
            pl.BlockSpec((tm, k), lambda j, i: (i, 0)),
            pl.BlockSpec((k, tn), lambda j, i: (0, j)),
            pl.BlockSpec((k, tn), lambda j, i: (0, j)),
            pl.BlockSpec((1, tn), lambda j, i: (0, j)),
            pl.BlockSpec((tm, tn), lambda j, i: (i, ga0 + j)),
            pl.BlockSpec((tm, tn), lambda j, i: (i, gb0 + j)),
        ],
        out_specs=pl.BlockSpec((tm, tn), lambda j, i: (i, j)),
        out_shape=jax.ShapeDtypeStruct((m, n), BF16),
        compiler_params=_cparams("parallel", "arbitrary"),
        name="merge_proj",
    )(c2, og, wc, wg, bc, gates, gates)


def _proj_ln_kernel(a_ref, w_ref, res_ref, g_ref, b_ref, o_ref, ob_ref, acc_ref, *, alpha):
    kk = pl.program_id(1)

    @pl.when(kk == 0)
    def _():
        acc_ref[...] = alpha * res_ref[...]

    acc_ref[...] += _dot(a_ref[...], w_ref[...])

    @pl.when(kk == pl.num_programs(1) - 1)
    def _():
        y = _layer_norm(acc_ref[...], g_ref[...], b_ref[...])
        o_ref[...] = y
        ob_ref[...] = y.astype(ob_ref.dtype)


def _proj_ln(a, w, res, g, b, *, alpha, tm, tk):
    m, k = a.shape
    n = w.shape[1]
    return pl.pallas_call(
        functools.partial(_proj_ln_kernel, alpha=alpha),
        grid=(m // tm, k // tk),
        in_specs=[
            pl.BlockSpec((tm, tk), lambda i, kk: (i, kk)),
            pl.BlockSpec((tk, n), lambda i, kk: (kk, 0)),
            pl.BlockSpec((tm, n), lambda i, kk: (i, 0)),
            pl.BlockSpec((1, n), lambda i, kk: (0, 0)),
            pl.BlockSpec((1, n), lambda i, kk: (0, 0)),
        ],
        out_specs=[
            pl.BlockSpec((tm, n), lambda i, kk: (i, 0)),
            pl.BlockSpec((tm, n), lambda i, kk: (i, 0)),
        ],
        out_shape=[jax.ShapeDtypeStruct((m, n), F32), jax.ShapeDtypeStruct((m, n), BF16)],
        scratch_shapes=[pltpu.VMEM((tm, n), F32)],
        compiler_params=_cparams("parallel", "arbitrary"),
        name="proj_ln",
    )(a, w, res, g, b)


def _swiglu_in_kernel(x_ref, wg_ref, wu_ref, o_ref):
    x = x_ref[...]
    gate = _dot(x, wg_ref[...])
    up = _dot(x, wu_ref[...])
    o_ref[...] = (_silu(gate) * up).astype(o_ref.dtype)


def _swiglu_in(xb, w, *, tm, tn):
    m, k = xb.shape
    d_ff = w.shape[1] // 2
    up0 = d_ff // tn
    return pl.pallas_call(
        _swiglu_in_kernel,
        grid=(d_ff // tn, m // tm),
        in_specs=[
            pl.BlockSpec((tm, k), lambda j, i: (i, 0)),
            pl.BlockSpec((k, tn), lambda j, i: (0, j)),
            pl.BlockSpec((k, tn), lambda j, i: (0, up0 + j)),
        ],
        out_specs=pl.BlockSpec((tm, tn), lambda j, i: (i, j)),
        out_shape=jax.ShapeDtypeStruct((m, d_ff), BF16),
        compiler_params=_cparams("parallel", "arbitrary"),
        name="swiglu_in",
    )(xb, w, w)


def _row(v):
    return v.reshape(1, -1).astype(F32)


def _pad_lanes(v, offset):
    return jnp.zeros((1, LANES), F32).at[0, offset:offset + v.shape[0]].set(v.astype(F32))


def _layer(x, xb, p, *, alpha):
    d_model = x.shape[1]
    n_heads = p["a_log"].shape[0]
    key_dim = n_heads * HEAD_DIM
    w_in, b_in = p["w_in"], p["b_in"]
    o_glu_b = d_model
    o_qkv = 2 * d_model
    o_z = o_qkv + 3 * key_dim
    o_beta = o_z + key_dim
    o_gate = o_beta + 2 * n_heads

    def wcols(lo, hi):
        return w_in[:, lo:hi].astype(BF16)

    c = _glu_proj(xb, wcols(0, o_glu_b), wcols(o_glu_b, o_qkv),
                  _row(b_in[0:o_glu_b]), _row(b_in[o_glu_b:o_qkv]), tm=512, tn=512)
    c2 = _conv_ln(c, p["conv_dw_w"], _row(p["conv_dw_b"]), _row(p["conv_ln_g"]),
                  _row(p["conv_ln_b"]), ts=256)

    qkv = _linear(xb, wcols(o_qkv, o_z), _row(b_in[o_qkv:o_z]), tm=512, tn=1024)
    qkvn = _qkv_prep(qkv, p["short_conv_w"], key_dim=key_dim, ts=512, tc=512)
    w_gates = jnp.concatenate([wcols(o_z, o_beta), wcols(o_gate, o_gate + 2 * d_model)], axis=1)
    b_gates = jnp.concatenate([b_in[o_z:o_beta], b_in[o_gate:o_gate + 2 * d_model]])
    gates = _gates_proj(xb, w_gates, _row(b_gates), n_silu_cols=key_dim, tm=512, tn=1024)
    w_bd = jnp.zeros((d_model, LANES), BF16).at[:, :2 * n_heads].set(wcols(o_beta, o_gate))
    bd, bdt = _beta_decay_proj(xb, w_bd, _pad_lanes(b_in[o_beta:o_gate], 0),
                               _pad_lanes(p["a_log"], n_heads), _pad_lanes(p["dt_bias"], n_heads),
                               n_heads=n_heads, tm=512)
    og = _gdn(qkvn, bd, bdt, gates, _row(p["gdn_norm_w"]), n_heads=n_heads, hg=4, ts=256, unit=128)

    m = _merge_proj(c2, og, p["w_conv_proj"].astype(BF16), p["w_gdn_proj"].astype(BF16),
                    _row(p["b_conv_proj"]), gates, gate_col0=key_dim, tm=512, tn=512)
    x1, x1b = _proj_ln(m, p["w_out"].astype(BF16), x, _row(p["ln1_g"]), _row(p["ln1_b"]),
                       alpha=alpha, tm=512, tk=d_model)
    hff = _swiglu_in(x1b, p["w_ffn_in"].astype(BF16), tm=512, tn=512)
    d_ff = hff.shape[1]
    x2, x2b = _proj_ln(hff, p["w_ffn_out"].astype(BF16), x1, _row(p["ln2_g"]), _row(p["ln2_b"]),
                       alpha=alpha, tm=512, tk=d_ff // 4)
    return x2, x2b


def kernel(x, w_in, b_in, conv_dw_w, conv_dw_b, conv_ln_g, conv_ln_b, w_conv_proj, b_conv_proj,
           short_conv_w, a_log, dt_bias, gdn_norm_w, w_gdn_proj, w_out, ln1_g, ln1_b,
           w_ffn_in, w_ffn_out, ln2_g, ln2_b):
    bsz, seq, d_model = x.shape
    depth = w_in.shape[0]
    alpha = (2.0 * depth) ** 0.25
    params = dict(w_in=w_in, b_in=b_in, conv_dw_w=conv_dw_w, conv_dw_b=conv_dw_b,
                  conv_ln_g=conv_ln_g, conv_ln_b=conv_ln_b, w_conv_proj=w_conv_proj,
                  b_conv_proj=b_conv_proj, short_conv_w=short_conv_w, a_log=a_log, dt_bias=dt_bias,
                  gdn_norm_w=gdn_norm_w, w_gdn_proj=w_gdn_proj, w_out=w_out, ln1_g=ln1_g,
                  ln1_b=ln1_b, w_ffn_in=w_ffn_in, w_ffn_out=w_ffn_out, ln2_g=ln2_g, ln2_b=ln2_b)
    outs = []
    for bi in range(bsz):
        h = x[bi]
        hb = h.astype(BF16)
        for l in range(depth):
            h, hb = _layer(h, hb, {k: v[l] for k, v in params.items()}, alpha=alpha)
        outs.append(h)
    return jnp.stack(outs, axis=0)
```

```python
import functools

import jax
import jax.numpy as jnp
from jax import lax
from jax.experimental import pallas as pl
from jax.experimental.pallas import tpu as pltpu

F32 = jnp.float32
BF16 = jnp.bfloat16

HEAD_DIM = 128
CONV_WIDTH = 31
SHORT_CONV = 4
CHUNK = 64
INV_BLOCK = 16
LN_EPS = 1e-5
LANES = 128
SUBLANES = 8
VMEM_LIMIT = 48 * 1024 * 1024
CAST_ROWS = 256


def _cparams(*sem):
    return pltpu.CompilerParams(dimension_semantics=sem, vmem_limit_bytes=VMEM_LIMIT)


def _dot(a, b):
    return jnp.dot(a, b, preferred_element_type=F32)


def _dot_nt(a, b):
    return lax.dot_general(a, b, (((1,), (1,)), ((), ())), preferred_element_type=F32)


def _dot_tn(a, b):
    return lax.dot_general(a, b, (((0,), (0,)), ((), ())), preferred_element_type=F32)


def _bf(x):
    return x.astype(BF16)


def _identity(x):
    return x


def _sigmoid(x):
    return 0.5 * jnp.tanh(0.5 * x) + 0.5


def _silu(x):
    hx = 0.5 * x
    return hx * jnp.tanh(hx) + hx


def _layer_norm(y, g, b):
    mu = jnp.mean(y, axis=-1, keepdims=True)
    d = y - mu
    var = jnp.mean(d * d, axis=-1, keepdims=True)
    return d * lax.rsqrt(var + LN_EPS) * g + b


def _cache_bf16(pairs):
    @pl.when(pl.program_id(1) == 0)
    def _():
        for w_ref, wb_ref in pairs:
            def body(r, carry, w_ref=w_ref, wb_ref=wb_ref):
                r0 = pl.multiple_of(r * CAST_ROWS, CAST_ROWS)
                wb_ref[pl.ds(r0, CAST_ROWS), :] = w_ref[pl.ds(r0, CAST_ROWS), :].astype(BF16)
                return carry

            lax.fori_loop(0, w_ref.shape[0] // CAST_ROWS, body, 0)


def _wspec(k, tn, layer, col_tile0):
    return pl.BlockSpec((None, k, tn), lambda j, i: (layer, 0, col_tile0 + j))


def _bspec(tn, layer, col_tile0):
    return pl.BlockSpec((None, 1, tn), lambda j, i: (layer, 0, col_tile0 + j))


def _glu_kernel(x_ref, wa_ref, wb_ref, ba_ref, bb_ref, o_ref, wab_ref, wbb_ref):
    _cache_bf16([(wa_ref, wab_ref), (wb_ref, wbb_ref)])
    for rows in _row_parts(x_ref.shape[0]):
        x = x_ref[rows, :]
        a = _dot(x, wab_ref[...]) + ba_ref[...]
        b = _dot(x, wbb_ref[...]) + bb_ref[...]
        o_ref[rows, :] = a * _sigmoid(b)


def _glu_proj(xb, w, b, layer, *, n, tm, tn):
    m, k = xb.shape
    return pl.pallas_call(
        _glu_kernel,
        grid=(n // tn, m // tm),
        in_specs=[
            pl.BlockSpec((tm, k), lambda j, i: (i, 0)),
            _wspec(k, tn, layer, 0), _wspec(k, tn, layer, n // tn),
            _bspec(tn, layer, 0), _bspec(tn, layer, n // tn),
        ],
        out_specs=pl.BlockSpec((tm, tn), lambda j, i: (i, j)),
        out_shape=jax.ShapeDtypeStruct((m, n), F32),
        scratch_shapes=[pltpu.VMEM((k, tn), BF16), pltpu.VMEM((k, tn), BF16)],
        compiler_params=_cparams("parallel", "arbitrary"),
        name="glu_proj",
    )(xb, w, w, b, b)


def _row_parts(tm):
    half = tm // 2
    return [slice(0, half), slice(half, tm)]


def _act_proj_kernel(x_ref, w_ref, b_ref, o_ref, wb_ref, *, act):
    _cache_bf16([(w_ref, wb_ref)])
    for rows in _row_parts(x_ref.shape[0]):
        acc = _dot(x_ref[rows, :], wb_ref[...]) + b_ref[...]
        o_ref[rows, :] = act(acc).astype(o_ref.dtype)


def _act_proj(xb, w, b, layer, *, col0, n, act, tm, tn, name):
    m, k = xb.shape
    return pl.pallas_call(
        functools.partial(_act_proj_kernel, act=act),
        grid=(n // tn, m // tm),
        in_specs=[
            pl.BlockSpec((tm, k), lambda j, i: (i, 0)),
            _wspec(k, tn, layer, col0 // tn), _bspec(tn, layer, col0 // tn),
        ],
        out_specs=pl.BlockSpec((tm, tn), lambda j, i: (i, j)),
        out_shape=jax.ShapeDtypeStruct((m, n), BF16),
        scratch_shapes=[pltpu.VMEM((k, tn), BF16)],
        compiler_params=_cparams("parallel", "arbitrary"),
        name=name,
    )(xb, w, b)


def _beta_decay_kernel(x_ref, w_ref, b_ref, a_ref, dt_ref, o_ref, ot_ref, *, n_heads):
    acc = _dot(x_ref[...], w_ref[...]) + b_ref[...]
    lane = lax.broadcasted_iota(jnp.int32, acc.shape, 1)
    beta = _sigmoid(acc)
    g = -jnp.exp(a_ref[...]) * jax.nn.softplus(acc + dt_ref[...])
    pos = lax.broadcasted_iota(jnp.int32, acc.shape, 0) % CHUNK
    shift = 1
    while shift < CHUNK:
        g = g + jnp.where(pos >= shift, pltpu.roll(g, shift, 0), 0.0)
        shift *= 2
    out = jnp.where(lane < n_heads, beta, g)
    o_ref[...] = out
    ot_ref[...] = out.T


def _beta_decay_proj(xb, w, b, a_log_row, dt_row, *, n_heads, tm):
    m, k = xb.shape
    return pl.pallas_call(
        functools.partial(_beta_decay_kernel, n_heads=n_heads),
        grid=(m // tm,),
        in_specs=[
            pl.BlockSpec((tm, k), lambda i: (i, 0)),
            pl.BlockSpec((k, LANES), lambda i: (0, 0)),
            pl.BlockSpec((1, LANES), lambda i: (0, 0)),
            pl.BlockSpec((1, LANES), lambda i: (0, 0)),
            pl.BlockSpec((1, LANES), lambda i: (0, 0)),
        ],
        out_specs=[pl.BlockSpec((tm, LANES), lambda i: (i, 0)),
                   pl.BlockSpec((LANES, tm), lambda i: (0, i))],
        out_shape=[jax.ShapeDtypeStruct((m, LANES), F32), jax.ShapeDtypeStruct((LANES, m), F32)],
        compiler_params=_cparams("parallel"),
        name="beta_decay_proj",
    )(xb, w, b, a_log_row, dt_row)


def _conv_ln_kernel(halo_ref, x_ref, w_ref, b_ref, g_ref, bn_ref, o_ref, xs_ref, y_ref, *,
                    ts, halo, rb, ln_rows):
    i = pl.program_id(0)
    ch = xs_ref.shape[1]

    @pl.when(i == 0)
    def _():
        xs_ref[0:halo, :] = jnp.zeros((halo, ch), F32)

    @pl.when(i > 0)
    def _():
        xs_ref[0:halo, :] = halo_ref[...]

    xs_ref[halo:halo + ts, :] = x_ref[...]
    n_rb = ts // rb

    def conv_block(idx, carry):
        c0 = pl.multiple_of((idx // n_rb) * LANES, LANES)
        r0 = (idx % n_rb) * rb
        lanes = pl.ds(c0, LANES)
        acc = jnp.zeros((rb, LANES), F32) + b_ref[:, lanes]
        for r in range(SUBLANES):
            lead = 0 if r == 0 else SUBLANES
            z = None
            for p in range((CONV_WIDTH - 1 - r) // SUBLANES + 1):
                j = CONV_WIDTH - 1 - (SUBLANES * p + r)
                row = pl.multiple_of(r0 + (halo - lead - SUBLANES * p), SUBLANES)
                term = xs_ref[pl.ds(row, rb + lead), lanes] * w_ref[j:j + 1, lanes]
                z = term if z is None else z + term
            acc = acc + (z if r == 0 else z[SUBLANES - r:SUBLANES - r + rb])
        y_ref[pl.ds(pl.multiple_of(r0, rb), rb), lanes] = acc
        return carry

    lax.fori_loop(0, n_rb * (ch // LANES), conv_block, 0)

    def ln_block(r, carry):
        r0 = pl.multiple_of(r * ln_rows, ln_rows)
        y = _layer_norm(y_ref[pl.ds(r0, ln_rows), :], g_ref[...], bn_ref[...])
        o_ref[pl.ds(r0, ln_rows), :] = _silu(y).astype(o_ref.dtype)
        return carry

    lax.fori_loop(0, ts // ln_rows, ln_block, 0)


def _conv_ln(c, w, b, g, bn, *, ts, rb, ln_rows):
    s, ch = c.shape
    halo = 4 * SUBLANES
    return pl.pallas_call(
        functools.partial(_conv_ln_kernel, ts=ts, halo=halo, rb=rb, ln_rows=ln_rows),
        grid=(s // ts,),
        in_specs=[
            pl.BlockSpec((halo, ch), lambda i: (jnp.maximum(i * (ts // halo) - 1, 0), 0)),
            pl.BlockSpec((ts, ch), lambda i: (i, 0)),
            pl.BlockSpec((CONV_WIDTH, ch), lambda i: (0, 0)),
            pl.BlockSpec((1, ch), lambda i: (0, 0)),
            pl.BlockSpec((1, ch), lambda i: (0, 0)),
            pl.BlockSpec((1, ch), lambda i: (0, 0)),
        ],
        out_specs=pl.BlockSpec((ts, ch), lambda i: (i, 0)),
        out_shape=jax.ShapeDtypeStruct((s, ch), BF16),
        scratch_shapes=[pltpu.VMEM((halo + ts, ch), F32), pltpu.VMEM((ts, ch), F32)],
        compiler_params=_cparams("parallel"),
        name="conv_ln",
    )(c, c, w, b, g, bn)


def _qkv_prep_kernel(halo_ref, x_ref, w_ref, o_ref, xs_ref, *, ts, halo, n_norm, q_scale):
    i = pl.program_id(1)
    j = pl.program_id(0)

    @pl.when(i == 0)
    def _():
        xs_ref[0:halo, :] = jnp.zeros((halo, xs_ref.shape[1]), F32)

    @pl.when(i > 0)
    def _():
        xs_ref[0:halo, :] = halo_ref[...].astype(F32)

    xs_ref[halo:halo + ts, :] = x_ref[...].astype(F32)
    lead = halo - (SHORT_CONV - 1)
    acc = xs_ref[lead:lead + ts, :] * w_ref[0:1, :]
    for t in range(1, SHORT_CONV):
        acc = acc + xs_ref[lead + t:lead + t + ts, :] * w_ref[t:t + 1, :]
    y = _silu(acc)
    n_heads_blk = y.shape[1] // HEAD_DIM
    is_qk = j < 2 * n_norm
    scale = jnp.where(j < n_norm, q_scale, 1.0).astype(F32)
    for h in range(n_heads_blk):
        yh = y[:, h * HEAD_DIM:(h + 1) * HEAD_DIM]
        nrm = yh * lax.rsqrt(jnp.sum(yh * yh, axis=-1, keepdims=True) + 1e-6) * scale
        o_ref[:, h * HEAD_DIM:(h + 1) * HEAD_DIM] = jnp.where(is_qk, nrm, yh).astype(o_ref.dtype)


def _qkv_prep(qkv, w, *, n, key_dim, ts, tc):
    s = qkv.shape[0]
    halo = 2 * SUBLANES
    return pl.pallas_call(
        functools.partial(_qkv_prep_kernel, ts=ts, halo=halo, n_norm=key_dim // tc,
                          q_scale=HEAD_DIM ** -0.5),
        grid=(n // tc, s // ts),
        in_specs=[
            pl.BlockSpec((halo, tc), lambda j, i: (jnp.maximum(i * (ts // halo) - 1, 0), j)),
            pl.BlockSpec((ts, tc), lambda j, i: (i, j)),
            pl.BlockSpec((SHORT_CONV, tc), lambda j, i: (0, j)),
        ],
        out_specs=pl.BlockSpec((ts, tc), lambda j, i: (i, j)),
        out_shape=jax.ShapeDtypeStruct((s, n), BF16),
        scratch_shapes=[pltpu.VMEM((halo + ts, tc), F32)],
        compiler_params=_cparams("parallel", "parallel"),
        name="qkv_prep",
    )(qkv, qkv, w)


def _gdn_kernel(q_ref, k_ref, v_ref, bd_ref, gt_ref, sz_ref, nw_ref, o_ref, s_ref, *,
                hg, ts, unit, n_heads):
    c = CHUNK
    hgrp = pl.program_id(0)

    @pl.when(pl.program_id(1) == 0)
    def _():
        s_ref[...] = jnp.zeros(s_ref.shape, F32)

    bd = bd_ref[...]
    lane = lax.broadcasted_iota(jnp.int32, (ts, LANES), 1)
    ri = lax.broadcasted_iota(jnp.int32, (unit, unit), 0)
    ci = lax.broadcasted_iota(jnp.int32, (unit, unit), 1)
    same_chunk = (ri // c) == (ci // c)
    causal = same_chunk & (ri >= ci)
    strict = same_chunk & (ri > ci)
    inv_block = (ri // INV_BLOCK) == (ci // INV_BLOCK)
    eye = (ri == ci).astype(F32)
    row_chunk = lax.broadcasted_iota(jnp.int32, (unit, 1), 0) // c
    nw = nw_ref[...]
    n_units = ts // unit
    cpu = unit // c
    cat = jnp.concatenate

    st = []
    for hl in range(hg):
        head = hgrp * hg + hl
        cols = slice(hl * HEAD_DIM, (hl + 1) * HEAD_DIM)
        beta_col = jnp.sum(jnp.where(lane == head, bd, 0.0), axis=1, keepdims=True)
        g_col = jnp.sum(jnp.where(lane == n_heads + head, bd, 0.0), axis=1, keepdims=True)
        g_row = gt_ref[pl.ds(n_heads + head, 1), :]
        for un in range(n_units):
            rows = slice(un * unit, (un + 1) * unit)
            gc, gr, bc = g_col[rows], g_row[:, rows], beta_col[rows]
            decay = jnp.where(causal, jnp.exp(jnp.where(causal, gc - gr, 0.0)), 0.0)
            qbf = q_ref[rows, cols]
            kbf = k_ref[rows, cols]
            k = kbf.astype(F32)
            kb = k * bc
            eg = jnp.exp(gc)
            gram = _dot_nt(cat([_bf(kb), qbf], axis=0), kbf)
            low = jnp.where(strict, gram[:unit] * decay, 0.0)
            dg = jnp.where(inv_block, low, 0.0)
            g_last = [gr[:, cc * c + c - 1:cc * c + c] for cc in range(cpu)]
            g_last_col = g_last[0]
            for cc in range(1, cpu):
                g_last_col = jnp.where(row_chunk == cc, g_last[cc], g_last_col)
            kd = _bf(k * jnp.exp(g_last_col - gc))
            st.append(dict(
                hl=hl, row0=un * unit, attn=_bf(gram[unit:] * decay), dg=dg, n=_bf(low - dg),
                p=eye - dg, rhs=_bf(cat([v_ref[rows, cols].astype(F32) * bc, kb * eg], axis=1)),
                qd=qbf.astype(F32) * eg, gamma=[jnp.exp(gl) for gl in g_last],
                kd=cat([jnp.where(row_chunk == cc, kd, jnp.zeros_like(kd)) for cc in range(cpu)], axis=1)))

    for d in st:
        dgb = _bf(d["dg"])
        d["pw"] = _dot(dgb, dgb)
    for _ in range(2):
        for d in st:
            pwb = _bf(d["pw"])
            r = _dot(cat([pwb, _bf(d["p"])], axis=0), pwb)
            d["pw"], d["p"] = r[:unit], d["p"] + r[unit:]
    for d in st:
        d["p"] = d["p"] + _dot(_bf(d["p"]), _bf(d["pw"]))
    for d in st:
        r = _dot(_bf(d["p"]), cat([d["n"], d["rhs"]], axis=1))
        d["m"], d["y"] = _bf(r[:, :unit]), r[:, unit:]
    for d in st:
        r = _dot(d["m"], cat([d["m"], _bf(d["y"])], axis=1))
        d["m2"], d["z"] = _bf(r[:, :unit]), d["y"] - r[:, unit:]
    for d in st:
        d["uw"] = _bf(d["z"] + _dot(d["m2"], _bf(d["z"])))
    for d in st:
        au = _dot(d["attn"], d["uw"])
        d["o_loc"] = au[:, :HEAD_DIM]
        d["q_eff"] = d["qd"] - au[:, HEAD_DIM:]
        d["bm"] = _dot_tn(d["kd"], d["uw"])

    states = [s_ref[hl] for hl in range(hg)]
    for un in range(n_units):
        for cc in range(cpu):
            for d in st:
                if d["row0"] != un * unit:
                    continue
                hl = d["hl"]
                cols = slice(hl * HEAD_DIM, (hl + 1) * HEAD_DIM)
                bm = d["bm"][cc * HEAD_DIM:(cc + 1) * HEAD_DIM]
                mq = cat([_bf(bm[:, HEAD_DIM:]), _bf(d["q_eff"][cc * c:(cc + 1) * c])], axis=0)
                ms_qs = _dot(mq, _bf(states[hl]))
                o = ms_qs[HEAD_DIM:] + d["o_loc"][cc * c:(cc + 1) * c]
                states[hl] = d["gamma"][cc] * states[hl] - ms_qs[:HEAD_DIM] + bm[:, :HEAD_DIM]
                rows = slice(d["row0"] + cc * c, d["row0"] + (cc + 1) * c)
                o = o * lax.rsqrt(jnp.mean(o * o, axis=-1, keepdims=True) + 1e-6)
                o = o * nw * sz_ref[rows, cols].astype(F32)
                o_ref[rows, cols] = o.astype(o_ref.dtype)
    for hl in range(hg):
        s_ref[hl] = states[hl]


def _gdn(qkvn, bd, bdt, qkvz, nw, *, n_heads, sz_col0, hg, ts, unit):
    s = qkvn.shape[0]
    width = hg * HEAD_DIM
    nblk = n_heads // hg
    sz0 = sz_col0 // width
    return pl.pallas_call(
        functools.partial(_gdn_kernel, hg=hg, ts=ts, unit=unit, n_heads=n_heads),
        grid=(nblk, s // ts),
        in_specs=[
            pl.BlockSpec((ts, width), lambda h, t: (t, h)),
            pl.BlockSpec((ts, width), lambda h, t: (t, nblk + h)),
            pl.BlockSpec((ts, width), lambda h, t: (t, 2 * nblk + h)),
            pl.BlockSpec((ts, LANES), lambda h, t: (t, 0)),
            pl.BlockSpec((LANES, ts), lambda h, t: (0, t)),
            pl.BlockSpec((ts, width), lambda h, t: (t, sz0 + h)),
            pl.BlockSpec((1, HEAD_DIM), lambda h, t: (0, 0)),
        ],
        out_specs=pl.BlockSpec((ts, width), lambda h, t: (t, h)),
        out_shape=jax.ShapeDtypeStruct((s, n_heads * HEAD_DIM), BF16),
        scratch_shapes=[pltpu.VMEM((hg, HEAD_DIM, HEAD_DIM), F32)],
        compiler_params=_cparams("parallel", "arbitrary"),
        name="gated_delta_rule",
    )(qkvn, qkvn, qkvn, bd, bdt, qkvz, nw)


def _merge_kernel(c_ref, og_ref, wc_ref, wg_ref, bc_ref, ga_ref, gb_ref, m_ref, wcb_ref, wgb_ref):
    _cache_bf16([(wc_ref, wcb_ref), (wg_ref, wgb_ref)])
    for rows in _row_parts(c_ref.shape[0]):
        yc = _dot(c_ref[rows, :], wcb_ref[...]) + bc_ref[...]
        yg = _dot(og_ref[rows, :], wgb_ref[...])
        m_ref[rows, :] = (ga_ref[rows, :] * yc + gb_ref[rows, :] * yg).astype(m_ref.dtype)


def _merge_proj(c2, og, wc, wg, bc, gates, layer, *, tm, tn):
    m, k = c2.shape
    n = wc.shape[2]
    return pl.pallas_call(
        _merge_kernel,
        grid=(n // tn, m // tm),
        in_specs=[
            pl.BlockSpec((tm, k), lambda j, i: (i, 0)),
            pl.BlockSpec((tm, k), lambda j, i: (i, 0)),
            _wspec(k, tn, layer, 0), _wspec(k, tn, layer, 0), _bspec(tn, layer, 0),
            pl.BlockSpec((tm, tn), lambda j, i: (i, j)),
            pl.BlockSpec((tm, tn), lambda j, i: (i, n // tn + j)),
        ],
        out_specs=pl.BlockSpec((tm, tn), lambda j, i: (i, j)),
        out_shape=jax.ShapeDtypeStruct((m, n), BF16),
        scratch_shapes=[pltpu.VMEM((k, tn), BF16), pltpu.VMEM((k, tn), BF16)],
        compiler_params=_cparams("parallel", "arbitrary"),
        name="merge_proj",
    )(c2, og, wc, wg, bc, gates, gates)


def _proj_ln_kernel(a_ref, w_ref, res_ref, g_ref, b_ref, o_ref, ob_ref, acc_ref, *, alpha, nk):
    kk = pl.program_id(1)
    tm = a_ref.shape[0]
    quarter = tm // 4
    parts = [slice(q * quarter, (q + 1) * quarter) for q in range(4)]

    def finish(partial_sum):
        for rows in parts:
            acc = partial_sum(rows) + _dot(a_ref[rows, :], w_ref[...])
            y = _layer_norm(acc, g_ref[...], b_ref[...])
            o_ref[rows, :] = y
            ob_ref[rows, :] = y.astype(ob_ref.dtype)

    if nk == 1:
        finish(lambda rows: alpha * res_ref[rows, :])
        return

    @pl.when(kk == 0)
    def _():
        acc_ref[...] = alpha * res_ref[...] + _dot(a_ref[...], w_ref[...])

    @pl.when((kk > 0) & (kk < nk - 1))
    def _():
        acc_ref[...] += _dot(a_ref[...], w_ref[...])

    @pl.when(kk == nk - 1)
    def _():
        finish(lambda rows: acc_ref[rows, :])


def _proj_ln(a, w, res, g, b, layer, *, alpha, tm, tk):
    m, k = a.shape
    n = w.shape[2]
    return pl.pallas_call(
        functools.partial(_proj_ln_kernel, alpha=alpha, nk=k // tk),
        grid=(m // tm, k // tk),
        in_specs=[
            pl.BlockSpec((tm, tk), lambda i, kk: (i, kk)),
            pl.BlockSpec((None, tk, n), lambda i, kk: (layer, kk, 0)),
            pl.BlockSpec((tm, n), lambda i, kk: (i, 0)),
            pl.BlockSpec((None, 1, n), lambda i, kk: (layer, 0, 0)),
            pl.BlockSpec((None, 1, n), lambda i, kk: (layer, 0, 0)),
        ],
        out_specs=[
            pl.BlockSpec((tm, n), lambda i, kk: (i, 0)),
            pl.BlockSpec((tm, n), lambda i, kk: (i, 0)),
        ],
        out_shape=[jax.ShapeDtypeStruct((m, n), F32), jax.ShapeDtypeStruct((m, n), BF16)],
        scratch_shapes=[pltpu.VMEM((tm, n), F32)],
        compiler_params=_cparams("parallel", "arbitrary"),
        name="proj_ln",
    )(a, w, res, g, b)


def _swiglu_in_kernel(x_ref, wg_ref, wu_ref, o_ref, wgb_ref, wub_ref):
    _cache_bf16([(wg_ref, wgb_ref), (wu_ref, wub_ref)])
    for rows in _row_parts(x_ref.shape[0]):
        x = x_ref[rows, :]
        gate = _dot(x, wgb_ref[...])
        up = _dot(x, wub_ref[...])
        o_ref[rows, :] = (_silu(gate) * up).astype(o_ref.dtype)


def _swiglu_in(xb, w, layer, *, tm, tn):
    m, k = xb.shape
    d_ff = w.shape[2] // 2
    return pl.pallas_call(
        _swiglu_in_kernel,
        grid=(d_ff // tn, m // tm),
        in_specs=[
            pl.BlockSpec((tm, k), lambda j, i: (i, 0)),
            _wspec(k, tn, layer, 0), _wspec(k, tn, layer, d_ff // tn),
        ],
        out_specs=pl.BlockSpec((tm, tn), lambda j, i: (i, j)),
        out_shape=jax.ShapeDtypeStruct((m, d_ff), BF16),
        scratch_shapes=[pltpu.VMEM((k, tn), BF16), pltpu.VMEM((k, tn), BF16)],
        compiler_params=_cparams("parallel", "arbitrary"),
        name="swiglu_in",
    )(xb, w, w)


def _rows3(v):
    return v.reshape(v.shape[0], 1, v.shape[1]).astype(F32)


def _pad_lanes(v, offset):
    return jnp.zeros((1, LANES), F32).at[0, offset:offset + v.shape[0]].set(v.astype(F32))


def _layer(x, xb, p, layer, *, alpha):
    d_model = x.shape[1]
    n_heads = p["a_log"].shape[1]
    key_dim = n_heads * HEAD_DIM
    o_qkv = 2 * d_model
    o_beta = o_qkv + 4 * key_dim

    c = _glu_proj(xb, p["w_in"], p["b_in"], layer, n=d_model, tm=512, tn=512)
    c2 = _conv_ln(c, p["conv_dw_w"][layer], p["conv_dw_b"][layer], p["conv_ln_g"][layer],
                  p["conv_ln_b"][layer], ts=256, rb=128, ln_rows=64)

    qkv = _act_proj(xb, p["w_in"], p["b_in"], layer, col0=o_qkv, n=3 * key_dim, act=_identity,
                    tm=512, tn=1024, name="qkv_proj")
    qkvn = _qkv_prep(qkv, p["short_conv_w"][layer], n=3 * key_dim, key_dim=key_dim, ts=512, tc=512)
    sz = _act_proj(xb, p["w_in"], p["b_in"], layer, col0=o_qkv + 3 * key_dim, n=key_dim, act=_silu,
                   tm=512, tn=1024, name="z_proj")
    gates = _act_proj(xb, p["w_gates"], p["b_gates"], layer, col0=0, n=2 * d_model, act=_sigmoid,
                      tm=512, tn=1024, name="gates_proj")
    bd, bdt = _beta_decay_proj(xb, p["w_bd"][layer], p["b_bd"][layer], p["a_log_row"][layer],
                               p["dt_row"][layer], n_heads=n_heads, tm=512)
    og = _gdn(qkvn, bd, bdt, sz, p["gdn_norm_w"][layer], n_heads=n_heads, sz_col0=0,
              hg=8, ts=256, unit=128)

    m = _merge_proj(c2, og, p["w_conv_proj"], p["w_gdn_proj"], p["b_conv_proj"], gates, layer,
                    tm=512, tn=512)
    x1, x1b = _proj_ln(m, p["w_out"], x, p["ln1_g"], p["ln1_b"], layer, alpha=alpha, tm=512,
                       tk=d_model)
    hff = _swiglu_in(x1b, p["w_ffn_in"], layer, tm=512, tn=512)
    x2, x2b = _proj_ln(hff, p["w_ffn_out"], x1, p["ln2_g"], p["ln2_b"], layer, alpha=alpha, tm=512,
                       tk=hff.shape[1] // 4)
    return x2, x2b


def kernel(x, w_in, b_in, conv_dw_w, conv_dw_b, conv_ln_g, conv_ln_b, w_conv_proj, b_conv_proj,
           short_conv_w, a_log, dt_bias, gdn_norm_w, w_gdn_proj, w_out, ln1_g, ln1_b,
           w_ffn_in, w_ffn_out, ln2_g, ln2_b):
    bsz, seq, d_model = x.shape
    depth, n_heads = a_log.shape
    alpha = (2.0 * depth) ** 0.25
    o_beta = 2 * d_model + 4 * n_heads * HEAD_DIM
    o_gate = o_beta + 2 * n_heads
    w_bd = jnp.zeros((depth, d_model, LANES), BF16).at[:, :, :2 * n_heads].set(
        w_in[:, :, o_beta:o_gate].astype(BF16))
    pad = jnp.zeros((depth, 1, LANES - n_heads), F32)
    p = dict(
        w_in=w_in, b_in=_rows3(b_in),
        w_gates=w_in[:, :, o_gate:], b_gates=_rows3(b_in[:, o_gate:]),
        w_bd=w_bd,
        b_bd=jnp.concatenate([_rows3(b_in[:, o_beta:o_gate]),
                              jnp.zeros((depth, 1, LANES - 2 * n_heads), F32)], axis=2),
        a_log_row=jnp.concatenate([jnp.zeros((depth, 1, n_heads), F32), _rows3(a_log), pad[:, :, n_heads:]], axis=2),
        dt_row=jnp.concatenate([jnp.zeros((depth, 1, n_heads), F32), _rows3(dt_bias), pad[:, :, n_heads:]], axis=2),
        conv_dw_w=conv_dw_w, conv_dw_b=_rows3(conv_dw_b), conv_ln_g=_rows3(conv_ln_g),
        conv_ln_b=_rows3(conv_ln_b), w_conv_proj=w_conv_proj, b_conv_proj=_rows3(b_conv_proj),
        short_conv_w=short_conv_w, a_log=a_log, gdn_norm_w=_rows3(gdn_norm_w),
        w_gdn_proj=w_gdn_proj, w_out=w_out.astype(BF16), ln1_g=_rows3(ln1_g), ln1_b=_rows3(ln1_b),
        w_ffn_in=w_ffn_in, w_ffn_out=w_ffn_out.astype(BF16), ln2_g=_rows3(ln2_g), ln2_b=_rows3(ln2_b))
    outs = []
    for bi in range(bsz):
        h = x[bi]
        hb = h.astype(BF16)
        for layer in range(depth):
            h, hb = _layer(h, hb, p, layer, alpha=alpha)
        outs.append(h)
    return jnp.stack(outs, axis=0)
```

```python
import functools

import jax
import jax.numpy as jnp
from jax import lax
from jax.experimental import pallas as pl
from jax.experimental.pallas import tpu as pltpu

F32 = jnp.float32
BF16 = jnp.bfloat16

HEAD_DIM = 128
CONV_WIDTH = 31
SHORT_CONV = 4
CHUNK = 64
INV_BLOCK = 16
LN_EPS = 1e-5
LANES = 128
SUBLANES = 8
VMEM_LIMIT = 48 * 1024 * 1024
CAST_ROWS = 256
ROW_PART = 256


def _cparams(*sem):
    return pltpu.CompilerParams(dimension_semantics=sem, vmem_limit_bytes=VMEM_LIMIT)


def _dot(a, b):
    return jnp.dot(a, b, preferred_element_type=F32)


def _dot_nt(a, b):
    return lax.dot_general(a, b, (((1,), (1,)), ((), ())), preferred_element_type=F32)


def _dot_tn(a, b):
    return lax.dot_general(a, b, (((0,), (0,)), ((), ())), preferred_element_type=F32)


def _bf(x):
    return x.astype(BF16)


def _identity(x):
    return x


def _sigmoid(x):
    return 0.5 * jnp.tanh(0.5 * x) + 0.5


def _silu(x):
    hx = 0.5 * x
    return hx * jnp.tanh(hx) + hx


def _layer_norm(y, g, b):
    mu = jnp.mean(y, axis=-1, keepdims=True)
    d = y - mu
    var = jnp.mean(d * d, axis=-1, keepdims=True)
    return d * lax.rsqrt(var + LN_EPS) * g + b


def _cache_bf16(pairs):
    @pl.when(pl.program_id(1) == 0)
    def _():
        for w_ref, wb_ref in pairs:
            def body(r, carry, w_ref=w_ref, wb_ref=wb_ref):
                r0 = pl.multiple_of(r * CAST_ROWS, CAST_ROWS)
                wb_ref[pl.ds(r0, CAST_ROWS), :] = w_ref[pl.ds(r0, CAST_ROWS), :].astype(BF16)
                return carry

            lax.fori_loop(0, w_ref.shape[0] // CAST_ROWS, body, 0)


def _wspec(k, tn, layer, col_tile0):
    return pl.BlockSpec((None, k, tn), lambda j, i: (layer, 0, col_tile0 + j))


def _bspec(tn, layer, col_tile0):
    return pl.BlockSpec((None, 1, tn), lambda j, i: (layer, 0, col_tile0 + j))


def _glu_kernel(x_ref, wa_ref, wb_ref, ba_ref, bb_ref, o_ref, wab_ref, wbb_ref):
    _cache_bf16([(wa_ref, wab_ref), (wb_ref, wbb_ref)])
    for rows in _row_parts(x_ref.shape[0]):
        x = x_ref[rows, :]
        a = _dot(x, wab_ref[...]) + ba_ref[...]
        b = _dot(x, wbb_ref[...]) + bb_ref[...]
        o_ref[rows, :] = a * _sigmoid(b)


def _glu_proj(xb, w, b, layer, *, n, tm, tn):
    m, k = xb.shape
    return pl.pallas_call(
        _glu_kernel,
        grid=(n // tn, m // tm),
        in_specs=[
            pl.BlockSpec((tm, k), lambda j, i: (i, 0)),
            _wspec(k, tn, layer, 0), _wspec(k, tn, layer, n // tn),
            _bspec(tn, layer, 0), _bspec(tn, layer, n // tn),
        ],
        out_specs=pl.BlockSpec((tm, tn), lambda j, i: (i, j)),
        out_shape=jax.ShapeDtypeStruct((m, n), F32),
        scratch_shapes=[pltpu.VMEM((k, tn), BF16), pltpu.VMEM((k, tn), BF16)],
        compiler_params=_cparams("parallel", "arbitrary"),
        name="glu_proj",
    )(xb, w, w, b, b)


def _row_parts(tm):
    return [slice(r, r + ROW_PART) for r in range(0, tm, ROW_PART)]


def _act_proj_kernel(x_ref, w_ref, b_ref, o_ref, wb_ref, *, act):
    _cache_bf16([(w_ref, wb_ref)])
    for rows in _row_parts(x_ref.shape[0]):
        acc = _dot(x_ref[rows, :], wb_ref[...]) + b_ref[...]
        o_ref[rows, :] = act(acc).astype(o_ref.dtype)


def _shifted_proj_kernel(x_ref, w_ref, wn_ref, b_ref, bn_ref, o_ref, wb_ref, bs_ref, *, shift, act):
    tn = w_ref.shape[1]

    @pl.when(pl.program_id(1) == 0)
    def _():
        def body(r, carry):
            rows = pl.ds(pl.multiple_of(r * CAST_ROWS, CAST_ROWS), CAST_ROWS)
            full = jnp.concatenate([w_ref[rows, :], wn_ref[rows, :]], axis=1)
            wb_ref[rows, :] = full[:, shift:shift + tn].astype(BF16)
            return carry

        lax.fori_loop(0, w_ref.shape[0] // CAST_ROWS, body, 0)
        bs_ref[...] = jnp.concatenate([b_ref[...], bn_ref[...]], axis=1)[:, shift:shift + tn]

    for rows in _row_parts(x_ref.shape[0]):
        acc = _dot(x_ref[rows, :], wb_ref[...]) + bs_ref[...]
        o_ref[rows, :] = act(acc).astype(o_ref.dtype)


def _shifted_proj(xb, w, b, layer, *, col0, n, act, tm, tn, name):
    m, k = xb.shape
    shift = col0 % LANES
    t0 = (col0 - shift) // tn
    nx = tn // LANES
    return pl.pallas_call(
        functools.partial(_shifted_proj_kernel, shift=shift, act=act),
        grid=(n // tn, m // tm),
        in_specs=[
            pl.BlockSpec((tm, k), lambda j, i: (i, 0)),
            _wspec(k, tn, layer, t0),
            pl.BlockSpec((None, k, LANES), lambda j, i: (layer, 0, (t0 + j + 1) * nx)),
            _bspec(tn, layer, t0),
            pl.BlockSpec((None, 1, LANES), lambda j, i: (layer, 0, (t0 + j + 1) * nx)),
        ],
        out_specs=pl.BlockSpec((tm, tn), lambda j, i: (i, j)),
        out_shape=jax.ShapeDtypeStruct((m, n), BF16),
        scratch_shapes=[pltpu.VMEM((k, tn), BF16), pltpu.VMEM((1, tn), F32)],
        compiler_params=_cparams("parallel", "arbitrary"),
        name=name,
    )(xb, w, w, b, b)


def _qkv_kernel(x_ref, w_ref, b_ref, cw_ref, o_ref, wb_ref, tail_ref, *, n_q, n_qk, q_scale):
    _cache_bf16([(w_ref, wb_ref)])
    j = pl.program_id(0)
    tn = w_ref.shape[1]

    @pl.when(pl.program_id(1) == 0)
    def _():
        tail_ref[...] = jnp.zeros(tail_ref.shape, F32)

    is_qk = j < n_qk
    scale = jnp.where(j < n_q, q_scale, 1.0).astype(F32)
    prev = tail_ref[...]
    lead = SUBLANES - (SHORT_CONV - 1)
    for rows in _row_parts(x_ref.shape[0]):
        acc = _dot(x_ref[rows, :], wb_ref[...]) + b_ref[...]
        nr = acc.shape[0]
        full = jnp.concatenate([prev, acc], axis=0)
        prev = acc[nr - SUBLANES:]
        y = full[lead:lead + nr] * cw_ref[0:1, :]
        for t in range(1, SHORT_CONV):
            y = y + full[lead + t:lead + t + nr] * cw_ref[t:t + 1, :]
        y = _silu(y)
        for h in range(tn // HEAD_DIM):
            cols = slice(h * HEAD_DIM, (h + 1) * HEAD_DIM)
            yh = y[:, cols]
            nrm = yh * (lax.rsqrt(jnp.sum(yh * yh, axis=-1, keepdims=True) + 1e-6) * scale)
            o_ref[rows, cols] = jnp.where(is_qk, nrm, yh).astype(o_ref.dtype)
    tail_ref[...] = prev


def _qkv_proj(xb, w, b, cw, layer, *, col0, key_dim, n, tm, tn):
    m, k = xb.shape
    return pl.pallas_call(
        functools.partial(_qkv_kernel, n_q=key_dim // tn, n_qk=2 * key_dim // tn,
                          q_scale=HEAD_DIM ** -0.5),
        grid=(n // tn, m // tm),
        in_specs=[
            pl.BlockSpec((tm, k), lambda j, i: (i, 0)),
            _wspec(k, tn, layer, col0 // tn), _bspec(tn, layer, col0 // tn),
            pl.BlockSpec((None, SHORT_CONV, tn), lambda j, i: (layer, 0, j)),
        ],
        out_specs=pl.BlockSpec((tm, tn), lambda j, i: (i, j)),
        out_shape=jax.ShapeDtypeStruct((m, n), BF16),
        scratch_shapes=[pltpu.VMEM((k, tn), BF16), pltpu.VMEM((SUBLANES, tn), F32)],
        compiler_params=_cparams("parallel", "arbitrary"),
        name="qkv_proj",
    )(xb, w, b, cw)


def _act_proj(xb, w, b, layer, *, col0, n, act, tm, tn, name):
    m, k = xb.shape
    return pl.pallas_call(
        functools.partial(_act_proj_kernel, act=act),
        grid=(n // tn, m // tm),
        in_specs=[
            pl.BlockSpec((tm, k), lambda j, i: (i, 0)),
            _wspec(k, tn, layer, col0 // tn), _bspec(tn, layer, col0 // tn),
        ],
        out_specs=pl.BlockSpec((tm, tn), lambda j, i: (i, j)),
        out_shape=jax.ShapeDtypeStruct((m, n), BF16),
        scratch_shapes=[pltpu.VMEM((k, tn), BF16)],
        compiler_params=_cparams("parallel", "arbitrary"),
        name=name,
    )(xb, w, b)


def _beta_decay_kernel(x_ref, w_ref, b_ref, a_ref, dt_ref, o_ref, ot_ref, wb_ref, *, n_heads):
    @pl.when(pl.program_id(0) == 0)
    def _():
        keep = lax.broadcasted_iota(jnp.int32, (CAST_ROWS, LANES), 1) < 2 * n_heads

        def body(r, carry):
            rows = pl.ds(pl.multiple_of(r * CAST_ROWS, CAST_ROWS), CAST_ROWS)
            wb_ref[rows, :] = jnp.where(keep, w_ref[rows, :], 0.0).astype(BF16)
            return carry

        lax.fori_loop(0, w_ref.shape[0] // CAST_ROWS, body, 0)

    lane = lax.broadcasted_iota(jnp.int32, (x_ref.shape[0], LANES), 1)
    acc = _dot(x_ref[...], wb_ref[...]) + jnp.where(lane < 2 * n_heads, b_ref[...], 0.0)
    beta = _sigmoid(acc)
    g = -jnp.exp(a_ref[...]) * jax.nn.softplus(acc + dt_ref[...])
    pos = lax.broadcasted_iota(jnp.int32, acc.shape, 0) % CHUNK
    shift = 1
    while shift < CHUNK:
        g = g + jnp.where(pos >= shift, pltpu.roll(g, shift, 0), 0.0)
        shift *= 2
    out = jnp.where(lane < n_heads, beta, g)
    o_ref[...] = out
    ot_ref[...] = out.T


def _beta_decay_proj(xb, w, b, a_log_row, dt_row, layer, *, col0, n_heads, tm):
    m, k = xb.shape
    cblk = col0 // LANES
    return pl.pallas_call(
        functools.partial(_beta_decay_kernel, n_heads=n_heads),
        grid=(m // tm,),
        in_specs=[
            pl.BlockSpec((tm, k), lambda i: (i, 0)),
            pl.BlockSpec((None, k, LANES), lambda i: (layer, 0, cblk)),
            pl.BlockSpec((None, 1, LANES), lambda i: (layer, 0, cblk)),
            pl.BlockSpec((None, 1, LANES), lambda i: (layer, 0, 0)),
            pl.BlockSpec((None, 1, LANES), lambda i: (layer, 0, 0)),
        ],
        out_specs=[pl.BlockSpec((tm, LANES), lambda i: (i, 0)),
                   pl.BlockSpec((LANES, tm), lambda i: (0, i))],
        out_shape=[jax.ShapeDtypeStruct((m, LANES), F32), jax.ShapeDtypeStruct((LANES, m), F32)],
        scratch_shapes=[pltpu.VMEM((k, LANES), BF16)],
        compiler_params=_cparams("arbitrary"),
        name="beta_decay_proj",
    )(xb, w, b, a_log_row, dt_row)


def _conv_ln_kernel(halo_ref, x_ref, w_ref, b_ref, g_ref, bn_ref, o_ref, xs_ref, y_ref, *,
                    ts, halo, rb, ln_rows):
    i = pl.program_id(0)
    ch = xs_ref.shape[1]

    @pl.when(i == 0)
    def _():
        xs_ref[0:halo, :] = jnp.zeros((halo, ch), F32)

    @pl.when(i > 0)
    def _():
        xs_ref[0:halo, :] = halo_ref[...]

    xs_ref[halo:halo + ts, :] = x_ref[...]
    n_rb = ts // rb

    def conv_block(idx, carry):
        c0 = pl.multiple_of((idx // n_rb) * LANES, LANES)
        r0 = (idx % n_rb) * rb
        lanes = pl.ds(c0, LANES)
        acc = jnp.zeros((rb, LANES), F32) + b_ref[:, lanes]
        for r in range(SUBLANES):
            lead = 0 if r == 0 else SUBLANES
            z = None
            for p in range((CONV_WIDTH - 1 - r) // SUBLANES + 1):
                j = CONV_WIDTH - 1 - (SUBLANES * p + r)
                row = pl.multiple_of(r0 + (halo - lead - SUBLANES * p), SUBLANES)
                term = xs_ref[pl.ds(row, rb + lead), lanes] * w_ref[j:j + 1, lanes]
                z = term if z is None else z + term
            acc = acc + (z if r == 0 else z[SUBLANES - r:SUBLANES - r + rb])
        y_ref[pl.ds(pl.multiple_of(r0, rb), rb), lanes] = acc
        return carry

    lax.fori_loop(0, n_rb * (ch // LANES), conv_block, 0)

    def ln_block(r, carry):
        r0 = pl.multiple_of(r * ln_rows, ln_rows)
        y = _layer_norm(y_ref[pl.ds(r0, ln_rows), :], g_ref[...], bn_ref[...])
        o_ref[pl.ds(r0, ln_rows), :] = _silu(y).astype(o_ref.dtype)
        return carry

    lax.fori_loop(0, ts // ln_rows, ln_block, 0)


def _conv_ln(c, w, b, g, bn, *, ts, rb, ln_rows):
    s, ch = c.shape
    halo = 4 * SUBLANES
    return pl.pallas_call(
        functools.partial(_conv_ln_kernel, ts=ts, halo=halo, rb=rb, ln_rows=ln_rows),
        grid=(s // ts,),
        in_specs=[
            pl.BlockSpec((halo, ch), lambda i: (jnp.maximum(i * (ts // halo) - 1, 0), 0)),
            pl.BlockSpec((ts, ch), lambda i: (i, 0)),
            pl.BlockSpec((CONV_WIDTH, ch), lambda i: (0, 0)),
            pl.BlockSpec((1, ch), lambda i: (0, 0)),
            pl.BlockSpec((1, ch), lambda i: (0, 0)),
            pl.BlockSpec((1, ch), lambda i: (0, 0)),
        ],
        out_specs=pl.BlockSpec((ts, ch), lambda i: (i, 0)),
        out_shape=jax.ShapeDtypeStruct((s, ch), BF16),
        scratch_shapes=[pltpu.VMEM((halo + ts, ch), F32), pltpu.VMEM((ts, ch), F32)],
        compiler_params=_cparams("parallel"),
        name="conv_ln",
    )(c, c, w, b, g, bn)


def _gdn_kernel(q_ref, k_ref, v_ref, bd_ref, gt_ref, sz_ref, nw_ref, o_ref, s_ref, *,
                hg, ts, unit, n_heads):
    c = CHUNK
    hgrp = pl.program_id(0)

    @pl.when(pl.program_id(1) == 0)
    def _():
        s_ref[...] = jnp.zeros(s_ref.shape, F32)

    bd = bd_ref[...]
    lane = lax.broadcasted_iota(jnp.int32, (ts, LANES), 1)
    ri = lax.broadcasted_iota(jnp.int32, (unit, unit), 0)
    ci = lax.broadcasted_iota(jnp.int32, (unit, unit), 1)
    same_chunk = (ri // c) == (ci // c)
    causal = same_chunk & (ri >= ci)
    strict = same_chunk & (ri > ci)
    inv_block = (ri // INV_BLOCK) == (ci // INV_BLOCK)
    eye = (ri == ci).astype(F32)
    row_chunk = lax.broadcasted_iota(jnp.int32, (unit, 1), 0) // c
    nw = nw_ref[...]
    n_units = ts // unit
    cpu = unit // c
    cat = jnp.concatenate

    st = []
    for hl in range(hg):
        head = hgrp * hg + hl
        cols = slice(hl * HEAD_DIM, (hl + 1) * HEAD_DIM)
        beta_col = jnp.sum(jnp.where(lane == head, bd, 0.0), axis=1, keepdims=True)
        g_col = jnp.sum(jnp.where(lane == n_heads + head, bd, 0.0), axis=1, keepdims=True)
        g_row = gt_ref[pl.ds(n_heads + head, 1), :]
        for un in range(n_units):
            rows = slice(un * unit, (un + 1) * unit)
            gc, gr, bc = g_col[rows], g_row[:, rows], beta_col[rows]
            decay = jnp.where(causal, jnp.exp(jnp.where(causal, gc - gr, 0.0)), 0.0)
            qbf = q_ref[rows, cols]
            kbf = k_ref[rows, cols]
            k = kbf.astype(F32)
            kb = k * bc
            eg = jnp.exp(gc)
            gram = _dot_nt(cat([_bf(kb), qbf], axis=0), kbf)
            low = jnp.where(strict, gram[:unit] * decay, 0.0)
            dg = jnp.where(inv_block, low, 0.0)
            g_last = [gr[:, cc * c + c - 1:cc * c + c] for cc in range(cpu)]
            g_last_col = g_last[0]
            for cc in range(1, cpu):
                g_last_col = jnp.where(row_chunk == cc, g_last[cc], g_last_col)
            kd = _bf(k * jnp.exp(g_last_col - gc))
            st.append(dict(
                hl=hl, row0=un * unit, attn=_bf(gram[unit:] * decay), dg=dg, n=_bf(low - dg),
                p=eye - dg, rhs=_bf(cat([v_ref[rows, cols].astype(F32) * bc, kb * eg], axis=1)),
                qd=qbf.astype(F32) * eg, gamma=[jnp.exp(gl) for gl in g_last],
                kd=cat([jnp.where(row_chunk == cc, kd, jnp.zeros_like(kd)) for cc in range(cpu)], axis=1)))

    for d in st:
        dgb = _bf(d["dg"])
        d["pw"] = _dot(dgb, dgb)
    for _ in range(2):
        for d in st:
            pwb = _bf(d["pw"])
            r = _dot(cat([pwb, _bf(d["p"])], axis=0), pwb)
            d["pw"], d["p"] = r[:unit], d["p"] + r[unit:]
    for d in st:
        d["p"] = d["p"] + _dot(_bf(d["p"]), _bf(d["pw"]))
    for d in st:
        r = _dot(_bf(d["p"]), cat([d["n"], d["rhs"]], axis=1))
        d["m"], d["y"] = _bf(r[:, :unit]), r[:, unit:]
    for d in st:
        r = _dot(d["m"], cat([d["m"], _bf(d["y"])], axis=1))
        d["m2"], d["z"] = _bf(r[:, :unit]), d["y"] - r[:, unit:]
    for d in st:
        d["uw"] = _bf(d["z"] + _dot(d["m2"], _bf(d["z"])))
    for d in st:
        au = _dot(d["attn"], d["uw"])
        d["o_loc"] = au[:, :HEAD_DIM]
        d["q_eff"] = d["qd"] - au[:, HEAD_DIM:]
        d["bm"] = _dot_tn(d["kd"], d["uw"])

    states = [s_ref[hl] for hl in range(hg)]
    for un in range(n_units):
        for cc in range(cpu):
            for d in st:
                if d["row0"] != un * unit:
                    continue
                hl = d["hl"]
                cols = slice(hl * HEAD_DIM, (hl + 1) * HEAD_DIM)
                bm = d["bm"][cc * HEAD_DIM:(cc + 1) * HEAD_DIM]
                mq = cat([_bf(bm[:, HEAD_DIM:]), _bf(d["q_eff"][cc * c:(cc + 1) * c])], axis=0)
                ms_qs = _dot(mq, _bf(states[hl]))
                o = ms_qs[HEAD_DIM:] + d["o_loc"][cc * c:(cc + 1) * c]
                states[hl] = d["gamma"][cc] * states[hl] - ms_qs[:HEAD_DIM] + bm[:, :HEAD_DIM]
                rows = slice(d["row0"] + cc * c, d["row0"] + (cc + 1) * c)
                o = o * lax.rsqrt(jnp.mean(o * o, axis=-1, keepdims=True) + 1e-6)
                o = o * nw * sz_ref[rows, cols].astype(F32)
                o_ref[rows, cols] = o.astype(o_ref.dtype)
    for hl in range(hg):
        s_ref[hl] = states[hl]


def _gdn(qkvn, bd, bdt, qkvz, nw, *, n_heads, sz_col0, hg, ts, unit):
    s = qkvn.shape[0]
    width = hg * HEAD_DIM
    nblk = n_heads // hg
    sz0 = sz_col0 // width
    return pl.pallas_call(
        functools.partial(_gdn_kernel, hg=hg, ts=ts, unit=unit, n_heads=n_heads),
        grid=(nblk, s // ts),
        in_specs=[
            pl.BlockSpec((ts, width), lambda h, t: (t, h)),
            pl.BlockSpec((ts, width), lambda h, t: (t, nblk + h)),
            pl.BlockSpec((ts, width), lambda h, t: (t, 2 * nblk + h)),
            pl.BlockSpec((ts, LANES), lambda h, t: (t, 0)),
            pl.BlockSpec((LANES, ts), lambda h, t: (0, t)),
            pl.BlockSpec((ts, width), lambda h, t: (t, sz0 + h)),
            pl.BlockSpec((1, HEAD_DIM), lambda h, t: (0, 0)),
        ],
        out_specs=pl.BlockSpec((ts, width), lambda h, t: (t, h)),
        out_shape=jax.ShapeDtypeStruct((s, n_heads * HEAD_DIM), BF16),
        scratch_shapes=[pltpu.VMEM((hg, HEAD_DIM, HEAD_DIM), F32)],
        compiler_params=_cparams("parallel", "arbitrary"),
        name="gated_delta_rule",
    )(qkvn, qkvn, qkvn, bd, bdt, qkvz, nw)


def _merge_kernel(c_ref, og_ref, wc_ref, wg_ref, bc_ref, ga_ref, gb_ref, m_ref, wcb_ref, wgb_ref):
    _cache_bf16([(wc_ref, wcb_ref), (wg_ref, wgb_ref)])
    for rows in _row_parts(c_ref.shape[0]):
        yc = _dot(c_ref[rows, :], wcb_ref[...]) + bc_ref[...]
        yg = _dot(og_ref[rows, :], wgb_ref[...])
        m_ref[rows, :] = (ga_ref[rows, :] * yc + gb_ref[rows, :] * yg).astype(m_ref.dtype)


def _merge_proj(c2, og, wc, wg, bc, gates, layer, *, tm, tn):
    m, k = c2.shape
    n = wc.shape[2]
    return pl.pallas_call(
        _merge_kernel,
        grid=(n // tn, m // tm),
        in_specs=[
            pl.BlockSpec((tm, k), lambda j, i: (i, 0)),
            pl.BlockSpec((tm, k), lambda j, i: (i, 0)),
            _wspec(k, tn, layer, 0), _wspec(k, tn, layer, 0), _bspec(tn, layer, 0),
            pl.BlockSpec((tm, tn), lambda j, i: (i, j)),
            pl.BlockSpec((tm, tn), lambda j, i: (i, n // tn + j)),
        ],
        out_specs=pl.BlockSpec((tm, tn), lambda j, i: (i, j)),
        out_shape=jax.ShapeDtypeStruct((m, n), BF16),
        scratch_shapes=[pltpu.VMEM((k, tn), BF16), pltpu.VMEM((k, tn), BF16)],
        compiler_params=_cparams("parallel", "arbitrary"),
        name="merge_proj",
    )(c2, og, wc, wg, bc, gates, gates)


def _proj_ln_kernel(a_ref, w_ref, res_ref, g_ref, b_ref, o_ref, ob_ref, acc_ref, *, alpha, nk):
    kk = pl.program_id(1)
    tm = a_ref.shape[0]
    quarter = tm // 4
    parts = [slice(q * quarter, (q + 1) * quarter) for q in range(4)]

    def finish(partial_sum):
        for rows in parts:
            acc = partial_sum(rows) + _dot(a_ref[rows, :], w_ref[...])
            y = _layer_norm(acc, g_ref[...], b_ref[...])
            o_ref[rows, :] = y
            ob_ref[rows, :] = y.astype(ob_ref.dtype)

    if nk == 1:
        finish(lambda rows: alpha * res_ref[rows, :])
        return

    @pl.when(kk == 0)
    def _():
        acc_ref[...] = alpha * res_ref[...] + _dot(a_ref[...], w_ref[...])

    @pl.when((kk > 0) & (kk < nk - 1))
    def _():
        acc_ref[...] += _dot(a_ref[...], w_ref[...])

    @pl.when(kk == nk - 1)
    def _():
        finish(lambda rows: acc_ref[rows, :])


def _proj_ln(a, w, res, g, b, layer, *, alpha, tm, tk):
    m, k = a.shape
    n = w.shape[2]
    return pl.pallas_call(
        functools.partial(_proj_ln_kernel, alpha=alpha, nk=k // tk),
        grid=(m // tm, k // tk),
        in_specs=[
            pl.BlockSpec((tm, tk), lambda i, kk: (i, kk)),
            pl.BlockSpec((None, tk, n), lambda i, kk: (layer, kk, 0)),
            pl.BlockSpec((tm, n), lambda i, kk: (i, 0)),
            pl.BlockSpec((None, 1, n), lambda i, kk: (layer, 0, 0)),
            pl.BlockSpec((None, 1, n), lambda i, kk: (layer, 0, 0)),
        ],
        out_specs=[
            pl.BlockSpec((tm, n), lambda i, kk: (i, 0)),
            pl.BlockSpec((tm, n), lambda i, kk: (i, 0)),
        ],
        out_shape=[jax.ShapeDtypeStruct((m, n), F32), jax.ShapeDtypeStruct((m, n), BF16)],
        scratch_shapes=[pltpu.VMEM((tm, n), F32)],
        compiler_params=_cparams("parallel", "arbitrary"),
        name="proj_ln",
    )(a, w, res, g, b)


def _swiglu_in_kernel(x_ref, wg_ref, wu_ref, o_ref, wgb_ref, wub_ref):
    _cache_bf16([(wg_ref, wgb_ref), (wu_ref, wub_ref)])
    for rows in _row_parts(x_ref.shape[0]):
        x = x_ref[rows, :]
        gate = _dot(x, wgb_ref[...])
        up = _dot(x, wub_ref[...])
        o_ref[rows, :] = (_silu(gate) * up).astype(o_ref.dtype)


def _swiglu_in(xb, w, layer, *, tm, tn):
    m, k = xb.shape
    d_ff = w.shape[2] // 2
    return pl.pallas_call(
        _swiglu_in_kernel,
        grid=(d_ff // tn, m // tm),
        in_specs=[
            pl.BlockSpec((tm, k), lambda j, i: (i, 0)),
            _wspec(k, tn, layer, 0), _wspec(k, tn, layer, d_ff // tn),
        ],
        out_specs=pl.BlockSpec((tm, tn), lambda j, i: (i, j)),
        out_shape=jax.ShapeDtypeStruct((m, d_ff), BF16),
        scratch_shapes=[pltpu.VMEM((k, tn), BF16), pltpu.VMEM((k, tn), BF16)],
        compiler_params=_cparams("parallel", "arbitrary"),
        name="swiglu_in",
    )(xb, w, w)


def _rows3(v):
    return v.reshape(v.shape[0], 1, v.shape[1]).astype(F32)


def _lane_row(v, offset):
    depth, n = v.shape
    return jnp.zeros((depth, 1, LANES), F32).at[:, 0, offset:offset + n].set(v.astype(F32))


TM = 1024
TN_ONE = 512
TN_TWO = 256
TM_LN = 512
TS_CONV, RB_CONV, LN_ROWS_CONV = 256, 128, 64
GDN_HEADS, TS_GDN, GDN_UNIT = 8, 256, 128


def _layer(x, xb, p, layer, *, alpha):
    d_model = x.shape[1]
    n_heads = p["n_heads"]
    key_dim = n_heads * HEAD_DIM
    o_qkv = 2 * d_model
    o_z = o_qkv + 3 * key_dim
    o_beta = o_z + key_dim
    o_gate = o_beta + 2 * n_heads
    tm = min(TM, x.shape[0])
    tm_ln = min(TM_LN, x.shape[0])

    c = _glu_proj(xb, p["w_in"], p["b_in"], layer, n=d_model, tm=tm, tn=TN_TWO)
    c2 = _conv_ln(c, p["conv_dw_w"][layer], p["conv_dw_b"][layer], p["conv_ln_g"][layer],
                  p["conv_ln_b"][layer], ts=TS_CONV, rb=RB_CONV, ln_rows=LN_ROWS_CONV)

    qkvn = _qkv_proj(xb, p["w_in"], p["b_in"], p["short_conv_w"], layer, col0=o_qkv,
                     key_dim=key_dim, n=3 * key_dim, tm=tm, tn=TN_ONE)
    sz = _act_proj(xb, p["w_in"], p["b_in"], layer, col0=o_z, n=key_dim, act=_silu,
                   tm=tm, tn=TN_ONE, name="z_proj")
    gates = _shifted_proj(xb, p["w_in"], p["b_in"], layer, col0=o_gate, n=2 * d_model,
                          act=_sigmoid, tm=tm, tn=TN_ONE, name="gates_proj")
    bd, bdt = _beta_decay_proj(xb, p["w_in"], p["b_in"], p["a_log_row"], p["dt_row"], layer,
                               col0=o_beta, n_heads=n_heads, tm=tm_ln)
    og = _gdn(qkvn, bd, bdt, sz, p["gdn_norm_w"][layer], n_heads=n_heads, sz_col0=0,
              hg=GDN_HEADS, ts=TS_GDN, unit=GDN_UNIT)

    m = _merge_proj(c2, og, p["w_conv_proj"], p["w_gdn_proj"], p["b_conv_proj"], gates, layer,
                    tm=tm, tn=TN_TWO)
    x1, x1b = _proj_ln(m, p["w_out"], x, p["ln1_g"], p["ln1_b"], layer, alpha=alpha, tm=tm_ln,
                       tk=d_model)
    hff = _swiglu_in(x1b, p["w_ffn_in"], layer, tm=tm, tn=TN_TWO)
    x2, x2b = _proj_ln(hff, p["w_ffn_out"], x1, p["ln2_g"], p["ln2_b"], layer, alpha=alpha,
                       tm=tm_ln, tk=hff.shape[1] // 4)
    return x2, x2b


def kernel(x, w_in, b_in, conv_dw_w, conv_dw_b, conv_ln_g, conv_ln_b, w_conv_proj, b_conv_proj,
           short_conv_w, a_log, dt_bias, gdn_norm_w, w_gdn_proj, w_out, ln1_g, ln1_b,
           w_ffn_in, w_ffn_out, ln2_g, ln2_b):
    bsz, seq, d_model = x.shape
    depth, n_heads = a_log.shape
    alpha = (2.0 * depth) ** 0.25
    p = dict(
        n_heads=n_heads, w_in=w_in, b_in=_rows3(b_in),
        a_log_row=_lane_row(a_log, n_heads), dt_row=_lane_row(dt_bias, n_heads),
        conv_dw_w=conv_dw_w, conv_dw_b=_rows3(conv_dw_b), conv_ln_g=_rows3(conv_ln_g),
        conv_ln_b=_rows3(conv_ln_b), w_conv_proj=w_conv_proj, b_conv_proj=_rows3(b_conv_proj),
        short_conv_w=short_conv_w, gdn_norm_w=_rows3(gdn_norm_w),
        w_gdn_proj=w_gdn_proj, w_out=w_out.astype(BF16), ln1_g=_rows3(ln1_g), ln1_b=_rows3(ln1_b),
        w_ffn_in=w_ffn_in, w_ffn_out=w_ffn_out.astype(BF16), ln2_g=_rows3(ln2_g), ln2_b=_rows3(ln2_b))
    outs = []
    for bi in range(bsz):
        h = x[bi]
        hb = h.astype(BF16)
        for layer in range(depth):
            h, hb = _layer(h, hb, p, layer, alpha=alpha)
        outs.append(h)
    return jnp.stack(outs, axis=0)
```

```python
import functools

import jax
import jax.numpy as jnp
from jax import lax
from jax.experimental import pallas as pl
from jax.experimental.pallas import tpu as pltpu

F32 = jnp.float32
BF16 = jnp.bfloat16

HEAD_DIM = 128
CONV_WIDTH = 31
SHORT_CONV = 4
CHUNK = 64
INV_BLOCK = 16
LN_EPS = 1e-5
LANES = 128
SUBLANES = 8
VMEM_LIMIT = 48 * 1024 * 1024
CAST_ROWS = 256
ROW_PART = 256


def _cparams(*sem):
    return pltpu.CompilerParams(dimension_semantics=sem, vmem_limit_bytes=VMEM_LIMIT)


def _dot(a, b):
    return jnp.dot(a, b, preferred_element_type=F32)


def _dot_nt(a, b):
    return lax.dot_general(a, b, (((1,), (1,)), ((), ())), preferred_element_type=F32)


def _dot_tn(a, b):
    return lax.dot_general(a, b, (((0,), (0,)), ((), ())), preferred_element_type=F32)


def _bf(x):
    return x.astype(BF16)


def _identity(x):
    return x


def _sigmoid(x):
    return 0.5 * jnp.tanh(0.5 * x) + 0.5


def _silu(x):
    hx = 0.5 * x
    return hx * jnp.tanh(hx) + hx


def _layer_norm(y, g, b):
    mu = jnp.mean(y, axis=-1, keepdims=True)
    d = y - mu
    var = jnp.mean(d * d, axis=-1, keepdims=True)
    return d * lax.rsqrt(var + LN_EPS) * g + b


def _cache_bf16(pairs):
    @pl.when(pl.program_id(1) == 0)
    def _():
        for w_ref, wb_ref in pairs:
            def body(r, carry, w_ref=w_ref, wb_ref=wb_ref):
                r0 = pl.multiple_of(r * CAST_ROWS, CAST_ROWS)
                wb_ref[pl.ds(r0, CAST_ROWS), :] = w_ref[pl.ds(r0, CAST_ROWS), :].astype(BF16)
                return carry

            lax.fori_loop(0, w_ref.shape[0] // CAST_ROWS, body, 0)


def _wspec(k, tn, layer, col_tile0):
    return pl.BlockSpec((None, k, tn), lambda j, i: (layer, 0, col_tile0 + j))


def _wtspec(k, tn, layer, row_tile0):
    return pl.BlockSpec((None, tn, k), lambda j, i: (layer, row_tile0 + j, 0))


def _bspec(tn, layer, col_tile0):
    return pl.BlockSpec((None, 1, tn), lambda j, i: (layer, 0, col_tile0 + j))


def _glu_kernel(x_ref, wa_ref, wb_ref, ba_ref, bb_ref, o_ref, wab_ref, wbb_ref):
    _cache_bf16([(wa_ref, wab_ref), (wb_ref, wbb_ref)])
    for rows in _row_parts(x_ref.shape[0]):
        x = x_ref[rows, :]
        a = _dot_nt(x, wab_ref[...]) + ba_ref[...]
        b = _dot_nt(x, wbb_ref[...]) + bb_ref[...]
        o_ref[rows, :] = a * _sigmoid(b)


def _glu_proj(xb, wt, b, layer, *, n, tm, tn):
    m, k = xb.shape
    return pl.pallas_call(
        _glu_kernel,
        grid=(n // tn, m // tm),
        in_specs=[
            pl.BlockSpec((tm, k), lambda j, i: (i, 0)),
            _wtspec(k, tn, layer, 0), _wtspec(k, tn, layer, n // tn),
            _bspec(tn, layer, 0), _bspec(tn, layer, n // tn),
        ],
        out_specs=pl.BlockSpec((tm, tn), lambda j, i: (i, j)),
        out_shape=jax.ShapeDtypeStruct((m, n), F32),
        scratch_shapes=[pltpu.VMEM((tn, k), BF16), pltpu.VMEM((tn, k), BF16)],
        compiler_params=_cparams("parallel", "arbitrary"),
        name="glu_proj",
    )(xb, wt, wt, b, b)


def _row_parts(tm):
    return [slice(r, r + ROW_PART) for r in range(0, tm, ROW_PART)]


def _act_proj_kernel(x_ref, w_ref, b_ref, o_ref, wb_ref, *, act):
    _cache_bf16([(w_ref, wb_ref)])
    for rows in _row_parts(x_ref.shape[0]):
        acc = _dot_nt(x_ref[rows, :], wb_ref[...]) + b_ref[...]
        o_ref[rows, :] = act(acc).astype(o_ref.dtype)


def _shifted_proj_kernel(x_ref, w_ref, wn_ref, b_ref, bn_ref, o_ref, wb_ref, bs_ref, *, shift, act):
    tn = w_ref.shape[0]

    @pl.when(pl.program_id(1) == 0)
    def _():
        wb_ref[0:tn - shift, :] = w_ref[shift:tn, :].astype(BF16)
        wb_ref[tn - shift:tn, :] = wn_ref[...].astype(BF16)
        bs_ref[...] = jnp.concatenate([b_ref[...], bn_ref[...]], axis=1)[:, shift:shift + tn]

    for rows in _row_parts(x_ref.shape[0]):
        acc = _dot_nt(x_ref[rows, :], wb_ref[...]) + bs_ref[...]
        o_ref[rows, :] = act(acc).astype(o_ref.dtype)


def _shifted_proj(xb, wt, b, layer, *, col0, n, act, tm, tn, name):
    m, k = xb.shape
    shift = col0 % LANES
    t0 = (col0 - shift) // tn
    return pl.pallas_call(
        functools.partial(_shifted_proj_kernel, shift=shift, act=act),
        grid=(n // tn, m // tm),
        in_specs=[
            pl.BlockSpec((tm, k), lambda j, i: (i, 0)),
            _wtspec(k, tn, layer, t0),
            pl.BlockSpec((None, shift, k), lambda j, i: (layer, (t0 + j + 1) * (tn // shift), 0)),
            _bspec(tn, layer, t0),
            pl.BlockSpec((None, 1, LANES), lambda j, i: (layer, 0, (t0 + j + 1) * (tn // LANES))),
        ],
        out_specs=pl.BlockSpec((tm, tn), lambda j, i: (i, j)),
        out_shape=jax.ShapeDtypeStruct((m, n), BF16),
        scratch_shapes=[pltpu.VMEM((tn, k), BF16), pltpu.VMEM((1, tn), F32)],
        compiler_params=_cparams("parallel", "arbitrary"),
        name=name,
    )(xb, wt, wt, b, b)


def _qkv_kernel(x_ref, w_ref, b_ref, cw_ref, o_ref, wb_ref, tail_ref, *, n_q, n_qk, q_scale):
    _cache_bf16([(w_ref, wb_ref)])
    j = pl.program_id(0)
    tn = w_ref.shape[0]

    @pl.when(pl.program_id(1) == 0)
    def _():
        tail_ref[...] = jnp.zeros(tail_ref.shape, F32)

    is_qk = j < n_qk
    scale = jnp.where(j < n_q, q_scale, 1.0).astype(F32)
    prev = tail_ref[...]
    half_cw = 0.5 * cw_ref[...]
    parts = _row_parts(x_ref.shape[0])
    dots = [_dot_nt(x_ref[parts[0], :], wb_ref[...])]
    for p, rows in enumerate(parts):
        if p + 1 < len(parts):
            dots.append(_dot_nt(x_ref[parts[p + 1], :], wb_ref[...]))
        acc = dots[p] + b_ref[...]
        nr = acc.shape[0]
        full = jnp.concatenate([prev, acc], axis=0)
        prev = acc[nr - SUBLANES:]
        hy = acc * half_cw[SHORT_CONV - 1:SHORT_CONV, :]
        for t in range(SHORT_CONV - 1):
            shifted = pltpu.roll(full, SHORT_CONV - 1 - t, 0)[SUBLANES:]
            hy = hy + shifted * half_cw[t:t + 1, :]
        y = hy * jnp.tanh(hy) + hy
        for h in range(tn // HEAD_DIM):
            cols = slice(h * HEAD_DIM, (h + 1) * HEAD_DIM)
            yh = y[:, cols]
            inv = lax.rsqrt(jnp.sum(yh * yh, axis=-1, keepdims=True) + 1e-6) * scale
            o_ref[rows, cols] = (yh * jnp.where(is_qk, inv, 1.0)).astype(o_ref.dtype)
    tail_ref[...] = prev


def _qkv_proj(xb, wt, b, cw, layer, *, col0, key_dim, n, tm, tn):
    m, k = xb.shape
    return pl.pallas_call(
        functools.partial(_qkv_kernel, n_q=key_dim // tn, n_qk=2 * key_dim // tn,
                          q_scale=HEAD_DIM ** -0.5),
        grid=(n // tn, m // tm),
        in_specs=[
            pl.BlockSpec((tm, k), lambda j, i: (i, 0)),
            _wtspec(k, tn, layer, col0 // tn), _bspec(tn, layer, col0 // tn),
            pl.BlockSpec((None, SHORT_CONV, tn), lambda j, i: (layer, 0, j)),
        ],
        out_specs=pl.BlockSpec((tm, tn), lambda j, i: (i, j)),
        out_shape=jax.ShapeDtypeStruct((m, n), BF16),
        scratch_shapes=[pltpu.VMEM((tn, k), BF16), pltpu.VMEM((SUBLANES, tn), F32)],
        compiler_params=_cparams("parallel", "arbitrary"),
        name="qkv_proj",
    )(xb, wt, b, cw)


def _act_proj(xb, wt, b, layer, *, col0, n, act, tm, tn, name):
    m, k = xb.shape
    return pl.pallas_call(
        functools.partial(_act_proj_kernel, act=act),
        grid=(n // tn, m // tm),
        in_specs=[
            pl.BlockSpec((tm, k), lambda j, i: (i, 0)),
            _wtspec(k, tn, layer, col0 // tn), _bspec(tn, layer, col0 // tn),
        ],
        out_specs=pl.BlockSpec((tm, tn), lambda j, i: (i, j)),
        out_shape=jax.ShapeDtypeStruct((m, n), BF16),
        scratch_shapes=[pltpu.VMEM((tn, k), BF16)],
        compiler_params=_cparams("parallel", "arbitrary"),
        name=name,
    )(xb, wt, b)


def _beta_decay_kernel(x_ref, w_ref, b_ref, a_ref, dt_ref, o_ref, ot_ref, wb_ref, *, n_heads):
    @pl.when(pl.program_id(0) == 0)
    def _():
        wb_ref[...] = jnp.zeros(wb_ref.shape, BF16)
        wb_ref[0:2 * n_heads, :] = w_ref[...].astype(BF16)

    lane = lax.broadcasted_iota(jnp.int32, (x_ref.shape[0], LANES), 1)
    acc = _dot_nt(x_ref[...], wb_ref[...]) + jnp.where(lane < 2 * n_heads, b_ref[...], 0.0)
    beta = _sigmoid(acc)
    g = -jnp.exp(a_ref[...]) * jax.nn.softplus(acc + dt_ref[...])
    pos = lax.broadcasted_iota(jnp.int32, acc.shape, 0) % CHUNK
    shift = 1
    while shift < CHUNK:
        g = g + jnp.where(pos >= shift, pltpu.roll(g, shift, 0), 0.0)
        shift *= 2
    out = jnp.where(lane < n_heads, beta, g)
    o_ref[...] = out
    ot_ref[...] = out.T


def _beta_decay_proj(xb, wt, b, a_log_row, dt_row, layer, *, col0, n_heads, tm):
    m, k = xb.shape
    cblk = col0 // LANES
    return pl.pallas_call(
        functools.partial(_beta_decay_kernel, n_heads=n_heads),
        grid=(m // tm,),
        in_specs=[
            pl.BlockSpec((tm, k), lambda i: (i, 0)),
            pl.BlockSpec((None, 2 * n_heads, k), lambda i: (layer, col0 // (2 * n_heads), 0)),
            pl.BlockSpec((None, 1, LANES), lambda i: (layer, 0, cblk)),
            pl.BlockSpec((None, 1, LANES), lambda i: (layer, 0, 0)),
            pl.BlockSpec((None, 1, LANES), lambda i: (layer, 0, 0)),
        ],
        out_specs=[pl.BlockSpec((tm, LANES), lambda i: (i, 0)),
                   pl.BlockSpec((LANES, tm), lambda i: (0, i))],
        out_shape=[jax.ShapeDtypeStruct((m, LANES), F32), jax.ShapeDtypeStruct((LANES, m), F32)],
        scratch_shapes=[pltpu.VMEM((LANES, k), BF16)],
        compiler_params=_cparams("arbitrary"),
        name="beta_decay_proj",
    )(xb, wt, b, a_log_row, dt_row)


def _conv_ln_kernel(halo_ref, x_ref, w_ref, b_ref, g_ref, bn_ref, o_ref, xs_ref, y_ref, *,
                    ts, halo, rb, ln_rows):
    i = pl.program_id(0)
    ch = xs_ref.shape[1]

    @pl.when(i == 0)
    def _():
        xs_ref[0:halo, :] = jnp.zeros((halo, ch), F32)

    @pl.when(i > 0)
    def _():
        xs_ref[0:halo, :] = halo_ref[...]

    xs_ref[halo:halo + ts, :] = x_ref[...]
    n_rb = ts // rb

    def conv_block(idx, carry):
        c0 = pl.multiple_of((idx // n_rb) * LANES, LANES)
        r0 = (idx % n_rb) * rb
        lanes = pl.ds(c0, LANES)
        acc = jnp.zeros((rb, LANES), F32) + b_ref[:, lanes]
        for r in range(SUBLANES):
            lead = 0 if r == 0 else SUBLANES
            z = None
            for p in range((CONV_WIDTH - 1 - r) // SUBLANES + 1):
                j = CONV_WIDTH - 1 - (SUBLANES * p + r)
                row = pl.multiple_of(r0 + (halo - lead - SUBLANES * p), SUBLANES)
                term = xs_ref[pl.ds(row, rb + lead), lanes] * w_ref[j:j + 1, lanes]
                z = term if z is None else z + term
            acc = acc + (z if r == 0 else z[SUBLANES - r:SUBLANES - r + rb])
        y_ref[pl.ds(pl.multiple_of(r0, rb), rb), lanes] = acc
        return carry

    lax.fori_loop(0, n_rb * (ch // LANES), conv_block, 0)

    def ln_block(r, carry):
        r0 = pl.multiple_of(r * ln_rows, ln_rows)
        y = _layer_norm(y_ref[pl.ds(r0, ln_rows), :], g_ref[...], bn_ref[...])
        o_ref[pl.ds(r0, ln_rows), :] = _silu(y).astype(o_ref.dtype)
        return carry

    lax.fori_loop(0, ts // ln_rows, ln_block, 0)


def _conv_ln(c, w, b, g, bn, *, ts, rb, ln_rows):
    s, ch = c.shape
    halo = 4 * SUBLANES
    return pl.pallas_call(
        functools.partial(_conv_ln_kernel, ts=ts, halo=halo, rb=rb, ln_rows=ln_rows),
        grid=(s // ts,),
        in_specs=[
            pl.BlockSpec((halo, ch), lambda i: (jnp.maximum(i * (ts // halo) - 1, 0), 0)),
            pl.BlockSpec((ts, ch), lambda i: (i, 0)),
            pl.BlockSpec((CONV_WIDTH, ch), lambda i: (0, 0)),
            pl.BlockSpec((1, ch), lambda i: (0, 0)),
            pl.BlockSpec((1, ch), lambda i: (0, 0)),
            pl.BlockSpec((1, ch), lambda i: (0, 0)),
        ],
        out_specs=pl.BlockSpec((ts, ch), lambda i: (i, 0)),
        out_shape=jax.ShapeDtypeStruct((s, ch), BF16),
        scratch_shapes=[pltpu.VMEM((halo + ts, ch), F32), pltpu.VMEM((ts, ch), F32)],
        compiler_params=_cparams("parallel"),
        name="conv_ln",
    )(c, c, w, b, g, bn)


def _gdn_kernel(q_ref, k_ref, v_ref, bd_ref, gt_ref, sz_ref, nw_ref, o_ref, s_ref, *,
                hg, ts, unit, n_heads):
    c = CHUNK
    hgrp = pl.program_id(0)

    @pl.when(pl.program_id(1) == 0)
    def _():
        s_ref[...] = jnp.zeros(s_ref.shape, F32)

    bd = bd_ref[...]
    lane = lax.broadcasted_iota(jnp.int32, (ts, LANES), 1)
    ri = lax.broadcasted_iota(jnp.int32, (unit, unit), 0)
    ci = lax.broadcasted_iota(jnp.int32, (unit, unit), 1)
    same_chunk = (ri // c) == (ci // c)
    causal = same_chunk & (ri >= ci)
    strict = same_chunk & (ri > ci)
    inv_block = (ri // INV_BLOCK) == (ci // INV_BLOCK)
    eye = (ri == ci).astype(F32)
    row_chunk = lax.broadcasted_iota(jnp.int32, (unit, 1), 0) // c
    nw = nw_ref[...]
    n_units = ts // unit
    cpu = unit // c
    cat = jnp.concatenate

    st = []
    for hl in range(hg):
        head = hgrp * hg + hl
        cols = slice(hl * HEAD_DIM, (hl + 1) * HEAD_DIM)
        beta_col = jnp.sum(jnp.where(lane == head, bd, 0.0), axis=1, keepdims=True)
        g_col = jnp.sum(jnp.where(lane == n_heads + head, bd, 0.0), axis=1, keepdims=True)
        g_row = gt_ref[pl.ds(n_heads + head, 1), :]
        for un in range(n_units):
            rows = slice(un * unit, (un + 1) * unit)
            gc, gr, bc = g_col[rows], g_row[:, rows], beta_col[rows]
            decay = jnp.where(causal, jnp.exp(jnp.where(causal, gc - gr, 0.0)), 0.0)
            qbf = q_ref[rows, cols]
            kbf = k_ref[rows, cols]
            k = kbf.astype(F32)
            kb = k * bc
            eg = jnp.exp(gc)
            gram = _dot_nt(cat([_bf(kb), qbf], axis=0), kbf)
            low = jnp.where(strict, gram[:unit] * decay, 0.0)
            dg = jnp.where(inv_block, low, 0.0)
            g_last = [gr[:, cc * c + c - 1:cc * c + c] for cc in range(cpu)]
            g_last_col = g_last[0]
            for cc in range(1, cpu):
                g_last_col = jnp.where(row_chunk == cc, g_last[cc], g_last_col)
            kd = _bf(k * jnp.exp(g_last_col - gc))
            st.append(dict(
                hl=hl, row0=un * unit, attn=_bf(gram[unit:] * decay), dg=dg, n=_bf(low - dg),
                p=eye - dg, rhs=_bf(cat([v_ref[rows, cols].astype(F32) * bc, kb * eg], axis=1)),
                qd=qbf.astype(F32) * eg, gamma=[jnp.exp(gl) for gl in g_last],
                kd=cat([jnp.where(row_chunk == cc, kd, jnp.zeros_like(kd)) for cc in range(cpu)], axis=1)))

    for d in st:
        dgb = _bf(d["dg"])
        d["pw"] = _dot(dgb, dgb)
    for _ in range(2):
        for d in st:
            pwb = _bf(d["pw"])
            r = _dot(cat([pwb, _bf(d["p"])], axis=0), pwb)
            d["pw"], d["p"] = r[:unit], d["p"] + r[unit:]
    for d in st:
        d["p"] = d["p"] + _dot(_bf(d["p"]), _bf(d["pw"]))
    for d in st:
        r = _dot(_bf(d["p"]), cat([d["n"], d["rhs"]], axis=1))
        d["m"], d["y"] = _bf(r[:, :unit]), r[:, unit:]
    for d in st:
        r = _dot(d["m"], cat([d["m"], _bf(d["y"])], axis=1))
        d["m2"], d["z"] = _bf(r[:, :unit]), d["y"] - r[:, unit:]
    for d in st:
        d["uw"] = _bf(d["z"] + _dot(d["m2"], _bf(d["z"])))
    for d in st:
        au = _dot(d["attn"], d["uw"])
        d["o_loc"] = au[:, :HEAD_DIM]
        d["q_eff"] = d["qd"] - au[:, HEAD_DIM:]
        d["bm"] = _dot_tn(d["kd"], d["uw"])

    states = [s_ref[hl] for hl in range(hg)]
    for un in range(n_units):
        for cc in range(cpu):
            for d in st:
                if d["row0"] != un * unit:
                    continue
                hl = d["hl"]
                cols = slice(hl * HEAD_DIM, (hl + 1) * HEAD_DIM)
                bm = d["bm"][cc * HEAD_DIM:(cc + 1) * HEAD_DIM]
                mq = cat([_bf(bm[:, HEAD_DIM:]), _bf(d["q_eff"][cc * c:(cc + 1) * c])], axis=0)
                ms_qs = _dot(mq, _bf(states[hl]))
                o = ms_qs[HEAD_DIM:] + d["o_loc"][cc * c:(cc + 1) * c]
                states[hl] = d["gamma"][cc] * states[hl] - ms_qs[:HEAD_DIM] + bm[:, :HEAD_DIM]
                rows = slice(d["row0"] + cc * c, d["row0"] + (cc + 1) * c)
                o = o * lax.rsqrt(jnp.mean(o * o, axis=-1, keepdims=True) + 1e-6)
                o = o * nw * sz_ref[rows, cols].astype(F32)
                o_ref[rows, cols] = o.astype(o_ref.dtype)
    for hl in range(hg):
        s_ref[hl] = states[hl]


def _gdn(qkvn, bd, bdt, qkvz, nw, *, n_heads, sz_col0, hg, ts, unit):
    s = qkvn.shape[0]
    width = hg * HEAD_DIM
    nblk = n_heads // hg
    sz0 = sz_col0 // width
    return pl.pallas_call(
        functools.partial(_gdn_kernel, hg=hg, ts=ts, unit=unit, n_heads=n_heads),
        grid=(nblk, s // ts),
        in_specs=[
            pl.BlockSpec((ts, width), lambda h, t: (t, h)),
            pl.BlockSpec((ts, width), lambda h, t: (t, nblk + h)),
            pl.BlockSpec((ts, width), lambda h, t: (t, 2 * nblk + h)),
            pl.BlockSpec((ts, LANES), lambda h, t: (t, 0)),
            pl.BlockSpec((LANES, ts), lambda h, t: (0, t)),
            pl.BlockSpec((ts, width), lambda h, t: (t, sz0 + h)),
            pl.BlockSpec((1, HEAD_DIM), lambda h, t: (0, 0)),
        ],
        out_specs=pl.BlockSpec((ts, width), lambda h, t: (t, h)),
        out_shape=jax.ShapeDtypeStruct((s, n_heads * HEAD_DIM), BF16),
        scratch_shapes=[pltpu.VMEM((hg, HEAD_DIM, HEAD_DIM), F32)],
        compiler_params=_cparams("parallel", "arbitrary"),
        name="gated_delta_rule",
    )(qkvn, qkvn, qkvn, bd, bdt, qkvz, nw)


def _merge_kernel(c_ref, og_ref, wc_ref, wg_ref, bc_ref, ga_ref, gb_ref, m_ref, wcb_ref, wgb_ref):
    _cache_bf16([(wc_ref, wcb_ref), (wg_ref, wgb_ref)])
    for rows in _row_parts(c_ref.shape[0]):
        yc = _dot(c_ref[rows, :], wcb_ref[...]) + bc_ref[...]
        yg = _dot(og_ref[rows, :], wgb_ref[...])
        m_ref[rows, :] = (ga_ref[rows, :] * yc + gb_ref[rows, :] * yg).astype(m_ref.dtype)


def _merge_proj(c2, og, wc, wg, bc, gates, layer, *, tm, tn):
    m, k = c2.shape
    n = wc.shape[2]
    return pl.pallas_call(
        _merge_kernel,
        grid=(n // tn, m // tm),
        in_specs=[
            pl.BlockSpec((tm, k), lambda j, i: (i, 0)),
            pl.BlockSpec((tm, k), lambda j, i: (i, 0)),
            _wspec(k, tn, layer, 0), _wspec(k, tn, layer, 0), _bspec(tn, layer, 0),
            pl.BlockSpec((tm, tn), lambda j, i: (i, j)),
            pl.BlockSpec((tm, tn), lambda j, i: (i, n // tn + j)),
        ],
        out_specs=pl.BlockSpec((tm, tn), lambda j, i: (i, j)),
        out_shape=jax.ShapeDtypeStruct((m, n), BF16),
        scratch_shapes=[pltpu.VMEM((k, tn), BF16), pltpu.VMEM((k, tn), BF16)],
        compiler_params=_cparams("parallel", "arbitrary"),
        name="merge_proj",
    )(c2, og, wc, wg, bc, gates, gates)


def _proj_ln_kernel(a_ref, w_ref, res_ref, g_ref, b_ref, o_ref, ob_ref, acc_ref, *, alpha, nk):
    kk = pl.program_id(1)
    parts = _row_parts(a_ref.shape[0])

    def finish(partial_sum):
        for rows in parts:
            acc = partial_sum(rows) + _dot(a_ref[rows, :], w_ref[...])
            y = _layer_norm(acc, g_ref[...], b_ref[...])
            o_ref[rows, :] = y
            ob_ref[rows, :] = y.astype(ob_ref.dtype)

    if nk == 1:
        finish(lambda rows: alpha * res_ref[rows, :])
        return

    @pl.when(kk == 0)
    def _():
        acc_ref[...] = alpha * res_ref[...] + _dot(a_ref[...], w_ref[...])

    @pl.when((kk > 0) & (kk < nk - 1))
    def _():
        acc_ref[...] += _dot(a_ref[...], w_ref[...])

    @pl.when(kk == nk - 1)
    def _():
        finish(lambda rows: acc_ref[rows, :])


def _proj_ln(a, w, res, g, b, layer, *, alpha, tm, tk):
    m, k = a.shape
    n = w.shape[2]
    nk = k // tk
    w_mode = dict(pipeline_mode=pl.Buffered(1)) if nk == 1 else {}
    return pl.pallas_call(
        functools.partial(_proj_ln_kernel, alpha=alpha, nk=nk),
        grid=(m // tm, nk),
        in_specs=[
            pl.BlockSpec((tm, tk), lambda i, kk: (i, kk)),
            pl.BlockSpec((None, tk, n), lambda i, kk: (layer, kk, 0), **w_mode),
            pl.BlockSpec((tm, n), lambda i, kk: (i, 0)),
            pl.BlockSpec((None, 1, n), lambda i, kk: (layer, 0, 0)),
            pl.BlockSpec((None, 1, n), lambda i, kk: (layer, 0, 0)),
        ],
        out_specs=[
            pl.BlockSpec((tm, n), lambda i, kk: (i, 0)),
            pl.BlockSpec((tm, n), lambda i, kk: (i, 0)),
        ],
        out_shape=[jax.ShapeDtypeStruct((m, n), F32), jax.ShapeDtypeStruct((m, n), BF16)],
        scratch_shapes=[pltpu.VMEM((tm if nk > 1 else SUBLANES, n), F32)],
        compiler_params=_cparams("parallel", "arbitrary"),
        name="proj_ln",
    )(a, w, res, g, b)


def _swiglu_in_kernel(x_ref, wg_ref, wu_ref, o_ref, wgb_ref, wub_ref):
    _cache_bf16([(wg_ref, wgb_ref), (wu_ref, wub_ref)])
    for rows in _row_parts(x_ref.shape[0]):
        x = x_ref[rows, :]
        gate = _dot(x, wgb_ref[...])
        up = _dot(x, wub_ref[...])
        o_ref[rows, :] = (_silu(gate) * up).astype(o_ref.dtype)


def _swiglu_in(xb, w, layer, *, tm, tn):
    m, k = xb.shape
    d_ff = w.shape[2] // 2
    return pl.pallas_call(
        _swiglu_in_kernel,
        grid=(d_ff // tn, m // tm),
        in_specs=[
            pl.BlockSpec((tm, k), lambda j, i: (i, 0)),
            _wspec(k, tn, layer, 0), _wspec(k, tn, layer, d_ff // tn),
        ],
        out_specs=pl.BlockSpec((tm, tn), lambda j, i: (i, j)),
        out_shape=jax.ShapeDtypeStruct((m, d_ff), BF16),
        scratch_shapes=[pltpu.VMEM((k, tn), BF16), pltpu.VMEM((k, tn), BF16)],
        compiler_params=_cparams("parallel", "arbitrary"),
        name="swiglu_in",
    )(xb, w, w)


def _rows3(v):
    return v.reshape(v.shape[0], 1, v.shape[1]).astype(F32)


def _lane_row(v, offset):
    depth, n = v.shape
    return jnp.zeros((depth, 1, LANES), F32).at[:, 0, offset:offset + n].set(v.astype(F32))


TM = 1024
TN_ONE = 512
TN_TWO = 512
TM_LN = 512
TS_CONV, RB_CONV, LN_ROWS_CONV = 256, 128, 64
GDN_HEADS, TS_GDN, GDN_UNIT = 8, 256, 128


def _layer(x, xb, p, layer, *, alpha):
    d_model = x.shape[1]
    n_heads = p["n_heads"]
    key_dim = n_heads * HEAD_DIM
    o_qkv = 2 * d_model
    o_z = o_qkv + 3 * key_dim
    o_beta = o_z + key_dim
    o_gate = o_beta + 2 * n_heads
    tm = min(TM, x.shape[0])
    tm_ln = min(TM_LN, x.shape[0])

    c = _glu_proj(xb, p["w_in"], p["b_in"], layer, n=d_model, tm=tm, tn=TN_TWO)
    c2 = _conv_ln(c, p["conv_dw_w"][layer], p["conv_dw_b"][layer], p["conv_ln_g"][layer],
                  p["conv_ln_b"][layer], ts=TS_CONV, rb=RB_CONV, ln_rows=LN_ROWS_CONV)

    qkvn = _qkv_proj(xb, p["w_in"], p["b_in"], p["short_conv_w"], layer, col0=o_qkv,
                     key_dim=key_dim, n=3 * key_dim, tm=tm, tn=TN_ONE)
    sz = _act_proj(xb, p["w_in"], p["b_in"], layer, col0=o_z, n=key_dim, act=_silu,
                   tm=tm, tn=TN_ONE, name="z_proj")
    gates = _shifted_proj(xb, p["w_in"], p["b_in"], layer, col0=o_gate, n=2 * d_model,
                          act=_sigmoid, tm=tm, tn=TN_ONE, name="gates_proj")
    bd, bdt = _beta_decay_proj(xb, p["w_in"], p["b_in"], p["a_log_row"], p["dt_row"], layer,
                               col0=o_beta, n_heads=n_heads, tm=tm_ln)
    og = _gdn(qkvn, bd, bdt, sz, p["gdn_norm_w"][layer], n_heads=n_heads, sz_col0=0,
              hg=GDN_HEADS, ts=TS_GDN, unit=GDN_UNIT)

    m = _merge_proj(c2, og, p["w_conv_proj"], p["w_gdn_proj"], p["b_conv_proj"], gates, layer,
                    tm=tm, tn=TN_ONE)
    x1, x1b = _proj_ln(m, p["w_out"], x, p["ln1_g"], p["ln1_b"], layer, alpha=alpha, tm=tm_ln,
                       tk=d_model)
    hff = _swiglu_in(x1b, p["w_ffn_in"], layer, tm=tm, tn=TN_TWO)
    x2, x2b = _proj_ln(hff, p["w_ffn_out"], x1, p["ln2_g"], p["ln2_b"], layer, alpha=alpha,
                       tm=tm_ln, tk=hff.shape[1] // 4)
    return x2, x2b


def kernel(x, w_in, b_in, conv_dw_w, conv_dw_b, conv_ln_g, conv_ln_b, w_conv_proj, b_conv_proj,
           short_conv_w, a_log, dt_bias, gdn_norm_w, w_gdn_proj, w_out, ln1_g, ln1_b,
           w_ffn_in, w_ffn_out, ln2_g, ln2_b):
    bsz, seq, d_model = x.shape
    depth, n_heads = a_log.shape
    alpha = (2.0 * depth) ** 0.25
    p = dict(
        n_heads=n_heads, w_in=jnp.swapaxes(w_in, 1, 2), b_in=_rows3(b_in),
        a_log_row=_lane_row(a_log, n_heads), dt_row=_lane_row(dt_bias, n_heads),
        conv_dw_w=conv_dw_w, conv_dw_b=_rows3(conv_dw_b), conv_ln_g=_rows3(conv_ln_g),
        conv_ln_b=_rows3(conv_ln_b), w_conv_proj=w_conv_proj, b_conv_proj=_rows3(b_conv_proj),
        short_conv_w=short_conv_w, gdn_norm_w=_rows3(gdn_norm_w),
        w_gdn_proj=w_gdn_proj, w_out=w_out.astype(BF16), ln1_g=_rows3(ln1_g), ln1_b=_rows3(ln1_b),
        w_ffn_in=w_ffn_in, w_ffn_out=w_ffn_out.astype(BF16), ln2_g=_rows3(ln2_g), ln2_b=_rows3(ln2_b))
    outs = []
    for bi in range(bsz):
        h = x[bi]
        hb = h.astype(BF16)
        for layer in range(depth):
            h, hb = _layer(h, hb, p, layer, alpha=alpha)
        outs.append(h)
    return jnp.stack(outs, axis=0)
```

```python
import functools

import jax
import jax.numpy as jnp
from jax import lax
from jax.experimental import pallas as pl
from jax.experimental.pallas import tpu as pltpu

F32 = jnp.float32
BF16 = jnp.bfloat16

HEAD_DIM = 128
CONV_WIDTH = 31
SHORT_CONV = 4
CHUNK = 64
INV_BLOCK = 16
LN_EPS = 1e-5
LANES = 128
SUBLANES = 8
VMEM_LIMIT = 48 * 1024 * 1024
CAST_ROWS = 256
ROW_PART = 256


def _cparams(*sem):
    return pltpu.CompilerParams(dimension_semantics=sem, vmem_limit_bytes=VMEM_LIMIT)


def _dot(a, b):
    return jnp.dot(a, b, preferred_element_type=F32)


def _dot_nt(a, b):
    return lax.dot_general(a, b, (((1,), (1,)), ((), ())), preferred_element_type=F32)


def _dot_tn(a, b):
    return lax.dot_general(a, b, (((0,), (0,)), ((), ())), preferred_element_type=F32)


def _bf(x):
    return x.astype(BF16)


def _identity(x):
    return x


def _sigmoid(x):
    return 0.5 * jnp.tanh(0.5 * x) + 0.5


def _silu(x):
    hx = 0.5 * x
    return hx * jnp.tanh(hx) + hx


def _layer_norm(y, g, b):
    mu = jnp.mean(y, axis=-1, keepdims=True)
    d = y - mu
    var = jnp.mean(d * d, axis=-1, keepdims=True)
    return d * lax.rsqrt(var + LN_EPS) * g + b


def _cache_bf16(pairs):
    @pl.when(pl.program_id(1) == 0)
    def _():
        for w_ref, wb_ref in pairs:
            def body(r, carry, w_ref=w_ref, wb_ref=wb_ref):
                r0 = pl.multiple_of(r * CAST_ROWS, CAST_ROWS)
                wb_ref[pl.ds(r0, CAST_ROWS), :] = w_ref[pl.ds(r0, CAST_ROWS), :].astype(BF16)
                return carry

            lax.fori_loop(0, w_ref.shape[0] // CAST_ROWS, body, 0)


def _wspec(k, tn, layer, col_tile0):
    return pl.BlockSpec((None, k, tn), lambda j, i: (layer, 0, col_tile0 + j))


def _wtspec(k, tn, layer, row_tile0):
    return pl.BlockSpec((None, tn, k), lambda j, i: (layer, row_tile0 + j, 0))


def _bspec(tn, layer, col_tile0):
    return pl.BlockSpec((None, 1, tn), lambda j, i: (layer, 0, col_tile0 + j))


def _glu_kernel(x_ref, wa_ref, wb_ref, ba_ref, bb_ref, o_ref, wab_ref, wbb_ref):
    _cache_bf16([(wa_ref, wab_ref), (wb_ref, wbb_ref)])
    for rows in _row_parts(x_ref.shape[0]):
        x = x_ref[rows, :]
        a = _dot_nt(x, wab_ref[...]) + ba_ref[...]
        b = _dot_nt(x, wbb_ref[...]) + bb_ref[...]
        o_ref[rows, :] = a * _sigmoid(b)


def _glu_proj(xb, wt, b, layer, *, n, tm, tn):
    m, k = xb.shape
    return pl.pallas_call(
        _glu_kernel,
        grid=(n // tn, m // tm),
        in_specs=[
            pl.BlockSpec((tm, k), lambda j, i: (i, 0)),
            _wtspec(k, tn, layer, 0), _wtspec(k, tn, layer, n // tn),
            _bspec(tn, layer, 0), _bspec(tn, layer, n // tn),
        ],
        out_specs=pl.BlockSpec((tm, tn), lambda j, i: (i, j)),
        out_shape=jax.ShapeDtypeStruct((m, n), F32),
        scratch_shapes=[pltpu.VMEM((tn, k), BF16), pltpu.VMEM((tn, k), BF16)],
        compiler_params=_cparams("parallel", "arbitrary"),
        name="glu_proj",
    )(xb, wt, wt, b, b)


def _row_parts(tm):
    return [slice(r, r + ROW_PART) for r in range(0, tm, ROW_PART)]


def _act_proj_kernel(x_ref, w_ref, b_ref, o_ref, wb_ref, *, act):
    _cache_bf16([(w_ref, wb_ref)])
    for rows in _row_parts(x_ref.shape[0]):
        acc = _dot_nt(x_ref[rows, :], wb_ref[...]) + b_ref[...]
        o_ref[rows, :] = act(acc).astype(o_ref.dtype)


def _shifted_proj_kernel(x_ref, w_ref, wn_ref, b_ref, bn_ref, o_ref, wb_ref, bs_ref, *, shift, act):
    tn = w_ref.shape[0]

    @pl.when(pl.program_id(1) == 0)
    def _():
        wb_ref[0:tn - shift, :] = w_ref[shift:tn, :].astype(BF16)
        wb_ref[tn - shift:tn, :] = wn_ref[...].astype(BF16)
        bs_ref[...] = jnp.concatenate([b_ref[...], bn_ref[...]], axis=1)[:, shift:shift + tn]

    for rows in _row_parts(x_ref.shape[0]):
        acc = _dot_nt(x_ref[rows, :], wb_ref[...]) + bs_ref[...]
        o_ref[rows, :] = act(acc).astype(o_ref.dtype)


def _shifted_proj(xb, wt, b, layer, *, col0, n, act, tm, tn, name):
    m, k = xb.shape
    shift = col0 % LANES
    t0 = (col0 - shift) // tn
    return pl.pallas_call(
        functools.partial(_shifted_proj_kernel, shift=shift, act=act),
        grid=(n // tn, m // tm),
        in_specs=[
            pl.BlockSpec((tm, k), lambda j, i: (i, 0)),
            _wtspec(k, tn, layer, t0),
            pl.BlockSpec((None, shift, k), lambda j, i: (layer, (t0 + j + 1) * (tn // shift), 0)),
            _bspec(tn, layer, t0),
            pl.BlockSpec((None, 1, LANES), lambda j, i: (layer, 0, (t0 + j + 1) * (tn // LANES))),
        ],
        out_specs=pl.BlockSpec((tm, tn), lambda j, i: (i, j)),
        out_shape=jax.ShapeDtypeStruct((m, n), BF16),
        scratch_shapes=[pltpu.VMEM((tn, k), BF16), pltpu.VMEM((1, tn), F32)],
        compiler_params=_cparams("parallel", "arbitrary"),
        name=name,
    )(xb, wt, wt, b, b)


def _qkv_kernel(x_ref, w_ref, b_ref, cw_ref, o_ref, wb_ref, tail_ref, *, n_q, n_qk, q_scale):
    _cache_bf16([(w_ref, wb_ref)])
    j = pl.program_id(0)
    tn = w_ref.shape[0]

    @pl.when(pl.program_id(1) == 0)
    def _():
        tail_ref[...] = jnp.zeros(tail_ref.shape, F32)

    is_qk = j < n_qk
    scale = jnp.where(j < n_q, q_scale, 1.0).astype(F32)
    prev = tail_ref[...]
    half_cw = 0.5 * cw_ref[...]
    parts = _row_parts(x_ref.shape[0])
    dots = [_dot_nt(x_ref[parts[0], :], wb_ref[...])]
    for p, rows in enumerate(parts):
        if p + 1 < len(parts):
            dots.append(_dot_nt(x_ref[parts[p + 1], :], wb_ref[...]))
        acc = dots[p] + b_ref[...]
        nr = acc.shape[0]
        full = jnp.concatenate([prev, acc], axis=0)
        prev = acc[nr - SUBLANES:]
        hy = acc * half_cw[SHORT_CONV - 1:SHORT_CONV, :]
        for t in range(SHORT_CONV - 1):
            shifted = pltpu.roll(full, SHORT_CONV - 1 - t, 0)[SUBLANES:]
            hy = hy + shifted * half_cw[t:t + 1, :]
        y = hy * jnp.tanh(hy) + hy
        for h in range(tn // HEAD_DIM):
            cols = slice(h * HEAD_DIM, (h + 1) * HEAD_DIM)
            yh = y[:, cols]
            inv = lax.rsqrt(jnp.sum(yh * yh, axis=-1, keepdims=True) + 1e-6) * scale
            o_ref[rows, cols] = (yh * jnp.where(is_qk, inv, 1.0)).astype(o_ref.dtype)
    tail_ref[...] = prev


def _qkv_proj(xb, wt, b, cw, layer, *, col0, key_dim, n, tm, tn):
    m, k = xb.shape
    return pl.pallas_call(
        functools.partial(_qkv_kernel, n_q=key_dim // tn, n_qk=2 * key_dim // tn,
                          q_scale=HEAD_DIM ** -0.5),
        grid=(n // tn, m // tm),
        in_specs=[
            pl.BlockSpec((tm, k), lambda j, i: (i, 0)),
            _wtspec(k, tn, layer, col0 // tn), _bspec(tn, layer, col0 // tn),
            pl.BlockSpec((None, SHORT_CONV, tn), lambda j, i: (layer, 0, j)),
        ],
        out_specs=pl.BlockSpec((tm, tn), lambda j, i: (i, j)),
        out_shape=jax.ShapeDtypeStruct((m, n), BF16),
        scratch_shapes=[pltpu.VMEM((tn, k), BF16), pltpu.VMEM((SUBLANES, tn), F32)],
        compiler_params=_cparams("parallel", "arbitrary"),
        name="qkv_proj",
    )(xb, wt, b, cw)


def _act_proj(xb, wt, b, layer, *, col0, n, act, tm, tn, name):
    m, k = xb.shape
    return pl.pallas_call(
        functools.partial(_act_proj_kernel, act=act),
        grid=(n // tn, m // tm),
        in_specs=[
            pl.BlockSpec((tm, k), lambda j, i: (i, 0)),
            _wtspec(k, tn, layer, col0 // tn), _bspec(tn, layer, col0 // tn),
        ],
        out_specs=pl.BlockSpec((tm, tn), lambda j, i: (i, j)),
        out_shape=jax.ShapeDtypeStruct((m, n), BF16),
        scratch_shapes=[pltpu.VMEM((tn, k), BF16)],
        compiler_params=_cparams("parallel", "arbitrary"),
        name=name,
    )(xb, wt, b)


def _beta_decay_kernel(x_ref, w_ref, b_ref, a_ref, dt_ref, o_ref, ot_ref, wb_ref, *, n_heads):
    @pl.when(pl.program_id(0) == 0)
    def _():
        wb_ref[...] = jnp.zeros(wb_ref.shape, BF16)
        wb_ref[0:2 * n_heads, :] = w_ref[...].astype(BF16)

    lane = lax.broadcasted_iota(jnp.int32, (x_ref.shape[0], LANES), 1)
    acc = _dot_nt(x_ref[...], wb_ref[...]) + jnp.where(lane < 2 * n_heads, b_ref[...], 0.0)
    beta = _sigmoid(acc)
    g = -jnp.exp(a_ref[...]) * jax.nn.softplus(acc + dt_ref[...])
    pos = lax.broadcasted_iota(jnp.int32, acc.shape, 0) % CHUNK
    shift = 1
    while shift < CHUNK:
        g = g + jnp.where(pos >= shift, pltpu.roll(g, shift, 0), 0.0)
        shift *= 2
    out = jnp.where(lane < n_heads, beta, g)
    o_ref[...] = out
    ot_ref[...] = out.T


def _beta_decay_proj(xb, wt, b, a_log_row, dt_row, layer, *, col0, n_heads, tm):
    m, k = xb.shape
    cblk = col0 // LANES
    return pl.pallas_call(
        functools.partial(_beta_decay_kernel, n_heads=n_heads),
        grid=(m // tm,),
        in_specs=[
            pl.BlockSpec((tm, k), lambda i: (i, 0)),
            pl.BlockSpec((None, 2 * n_heads, k), lambda i: (layer, col0 // (2 * n_heads), 0)),
            pl.BlockSpec((None, 1, LANES), lambda i: (layer, 0, cblk)),
            pl.BlockSpec((None, 1, LANES), lambda i: (layer, 0, 0)),
            pl.BlockSpec((None, 1, LANES), lambda i: (layer, 0, 0)),
        ],
        out_specs=[pl.BlockSpec((tm, LANES), lambda i: (i, 0)),
                   pl.BlockSpec((LANES, tm), lambda i: (0, i))],
        out_shape=[jax.ShapeDtypeStruct((m, LANES), F32), jax.ShapeDtypeStruct((LANES, m), F32)],
        scratch_shapes=[pltpu.VMEM((LANES, k), BF16)],
        compiler_params=_cparams("arbitrary"),
        name="beta_decay_proj",
    )(xb, wt, b, a_log_row, dt_row)


def _conv_ln_kernel(halo_ref, x_ref, w_ref, b_ref, g_ref, bn_ref, o_ref, xs_ref, y_ref, *,
                    ts, halo, rb, ln_rows):
    i = pl.program_id(0)
    ch = xs_ref.shape[1]

    @pl.when(i == 0)
    def _():
        xs_ref[0:halo, :] = jnp.zeros((halo, ch), F32)

    @pl.when(i > 0)
    def _():
        xs_ref[0:halo, :] = halo_ref[...]

    xs_ref[halo:halo + ts, :] = x_ref[...]
    n_rb = ts // rb

    def conv_block(idx, carry):
        c0 = pl.multiple_of((idx // n_rb) * LANES, LANES)
        r0 = (idx % n_rb) * rb
        lanes = pl.ds(c0, LANES)
        acc = jnp.zeros((rb, LANES), F32) + b_ref[:, lanes]
        for r in range(SUBLANES):
            lead = 0 if r == 0 else SUBLANES
            z = None
            for p in range((CONV_WIDTH - 1 - r) // SUBLANES + 1):
                j = CONV_WIDTH - 1 - (SUBLANES * p + r)
                row = pl.multiple_of(r0 + (halo - lead - SUBLANES * p), SUBLANES)
                term = xs_ref[pl.ds(row, rb + lead), lanes] * w_ref[j:j + 1, lanes]
                z = term if z is None else z + term
            acc = acc + (z if r == 0 else z[SUBLANES - r:SUBLANES - r + rb])
        y_ref[pl.ds(pl.multiple_of(r0, rb), rb), lanes] = acc
        return carry

    lax.fori_loop(0, n_rb * (ch // LANES), conv_block, 0)

    def ln_block(r, carry):
        r0 = pl.multiple_of(r * ln_rows, ln_rows)
        y = _layer_norm(y_ref[pl.ds(r0, ln_rows), :], g_ref[...], bn_ref[...])
        o_ref[pl.ds(r0, ln_rows), :] = _silu(y).astype(o_ref.dtype)
        return carry

    lax.fori_loop(0, ts // ln_rows, ln_block, 0)


def _conv_ln(c, w, b, g, bn, *, ts, rb, ln_rows):
    s, ch = c.shape
    halo = 4 * SUBLANES
    return pl.pallas_call(
        functools.partial(_conv_ln_kernel, ts=ts, halo=halo, rb=rb, ln_rows=ln_rows),
        grid=(s // ts,),
        in_specs=[
            pl.BlockSpec((halo, ch), lambda i: (jnp.maximum(i * (ts // halo) - 1, 0), 0)),
            pl.BlockSpec((ts, ch), lambda i: (i, 0)),
            pl.BlockSpec((CONV_WIDTH, ch), lambda i: (0, 0)),
            pl.BlockSpec((1, ch), lambda i: (0, 0)),
            pl.BlockSpec((1, ch), lambda i: (0, 0)),
            pl.BlockSpec((1, ch), lambda i: (0, 0)),
        ],
        out_specs=pl.BlockSpec((ts, ch), lambda i: (i, 0)),
        out_shape=jax.ShapeDtypeStruct((s, ch), BF16),
        scratch_shapes=[pltpu.VMEM((halo + ts, ch), F32), pltpu.VMEM((ts, ch), F32)],
        compiler_params=_cparams("parallel"),
        name="conv_ln",
    )(c, c, w, b, g, bn)


def _gdn_kernel(q_ref, k_ref, v_ref, bd_ref, gt_ref, sz_ref, nw_ref, o_ref, s_ref, *,
                hg, ts, unit, n_heads):
    c = CHUNK
    hgrp = pl.program_id(0)

    @pl.when(pl.program_id(1) == 0)
    def _():
        s_ref[...] = jnp.zeros(s_ref.shape, F32)

    bd = bd_ref[...]
    lane = lax.broadcasted_iota(jnp.int32, (ts, LANES), 1)
    ri = lax.broadcasted_iota(jnp.int32, (unit, unit), 0)
    ci = lax.broadcasted_iota(jnp.int32, (unit, unit), 1)
    same_chunk = (ri // c) == (ci // c)
    causal = same_chunk & (ri >= ci)
    strict = same_chunk & (ri > ci)
    inv_block = (ri // INV_BLOCK) == (ci // INV_BLOCK)
    eye = (ri == ci).astype(F32)
    row_chunk = lax.broadcasted_iota(jnp.int32, (unit, 1), 0) // c
    nw = nw_ref[...]
    n_units = ts // unit
    cpu = unit // c
    cat = jnp.concatenate

    st = []
    for hl in range(hg):
        head = hgrp * hg + hl
        cols = slice(hl * HEAD_DIM, (hl + 1) * HEAD_DIM)
        beta_col = jnp.sum(jnp.where(lane == head, bd, 0.0), axis=1, keepdims=True)
        g_col = jnp.sum(jnp.where(lane == n_heads + head, bd, 0.0), axis=1, keepdims=True)
        g_row = gt_ref[pl.ds(n_heads + head, 1), :]
        for un in range(n_units):
            rows = slice(un * unit, (un + 1) * unit)
            gc, gr, bc = g_col[rows], g_row[:, rows], beta_col[rows]
            decay = jnp.where(causal, jnp.exp(jnp.where(causal, gc - gr, 0.0)), 0.0)
            qbf = q_ref[rows, cols]
            kbf = k_ref[rows, cols]
            k = kbf.astype(F32)
            kb = k * bc
            eg = jnp.exp(gc)
            gram = _dot_nt(cat([_bf(kb), qbf], axis=0), kbf)
            low = jnp.where(strict, gram[:unit] * decay, 0.0)
            dg = jnp.where(inv_block, low, 0.0)
            g_last = [gr[:, cc * c + c - 1:cc * c + c] for cc in range(cpu)]
            g_last_col = g_last[0]
            for cc in range(1, cpu):
                g_last_col = jnp.where(row_chunk == cc, g_last[cc], g_last_col)
            kd = _bf(k * jnp.exp(g_last_col - gc))
            st.append(dict(
                hl=hl, row0=un * unit, attn=_bf(gram[unit:] * decay), dg=dg, n=_bf(low - dg),
                p=eye - dg, rhs=_bf(cat([v_ref[rows, cols].astype(F32) * bc, kb * eg], axis=1)),
                qd=qbf.astype(F32) * eg, gamma=[jnp.exp(gl) for gl in g_last],
                kd=cat([jnp.where(row_chunk == cc, kd, jnp.zeros_like(kd)) for cc in range(cpu)], axis=1)))

    for d in st:
        dgb = _bf(d["dg"])
        d["pw"] = _dot(dgb, dgb)
    for _ in range(2):
        for d in st:
            pwb = _bf(d["pw"])
            r = _dot(cat([pwb, _bf(d["p"])], axis=0), pwb)
            d["pw"], d["p"] = r[:unit], d["p"] + r[unit:]
    for d in st:
        d["p"] = d["p"] + _dot(_bf(d["p"]), _bf(d["pw"]))
    for d in st:
        r = _dot(_bf(d["p"]), cat([d["n"], d["rhs"]], axis=1))
        d["m"], d["y"] = _bf(r[:, :unit]), r[:, unit:]
    for d in st:
        r = _dot(d["m"], cat([d["m"], _bf(d["y"])], axis=1))
        d["m2"], d["z"] = _bf(r[:, :unit]), d["y"] - r[:, unit:]
    for d in st:
        d["uw"] = _bf(d["z"] + _dot(d["m2"], _bf(d["z"])))
    for d in st:
        au = _dot(d["attn"], d["uw"])
        d["o_loc"] = au[:, :HEAD_DIM]
        d["q_eff"] = d["qd"] - au[:, HEAD_DIM:]
        d["bm"] = _dot_tn(d["kd"], d["uw"])

    states = [s_ref[hl] for hl in range(hg)]
    for un in range(n_units):
        for cc in range(cpu):
            for d in st:
                if d["row0"] != un * unit:
                    continue
                hl = d["hl"]
                cols = slice(hl * HEAD_DIM, (hl + 1) * HEAD_DIM)
                bm = d["bm"][cc * HEAD_DIM:(cc + 1) * HEAD_DIM]
                mq = cat([_bf(bm[:, HEAD_DIM:]), _bf(d["q_eff"][cc * c:(cc + 1) * c])], axis=0)
                ms_qs = _dot(mq, _bf(states[hl]))
                o = ms_qs[HEAD_DIM:] + d["o_loc"][cc * c:(cc + 1) * c]
                states[hl] = d["gamma"][cc] * states[hl] - ms_qs[:HEAD_DIM] + bm[:, :HEAD_DIM]
                rows = slice(d["row0"] + cc * c, d["row0"] + (cc + 1) * c)
                o = o * lax.rsqrt(jnp.mean(o * o, axis=-1, keepdims=True) + 1e-6)
                o = o * nw * sz_ref[rows, cols].astype(F32)
                o_ref[rows, cols] = o.astype(o_ref.dtype)
    for hl in range(hg):
        s_ref[hl] = states[hl]


def _gdn(qkvn, bd, bdt, qkvz, nw, *, n_heads, sz_col0, hg, ts, unit):
    s = qkvn.shape[0]
    width = hg * HEAD_DIM
    nblk = n_heads // hg
    sz0 = sz_col0 // width
    return pl.pallas_call(
        functools.partial(_gdn_kernel, hg=hg, ts=ts, unit=unit, n_heads=n_heads),
        grid=(nblk, s // ts),
        in_specs=[
            pl.BlockSpec((ts, width), lambda h, t: (t, h)),
            pl.BlockSpec((ts, width), lambda h, t: (t, nblk + h)),
            pl.BlockSpec((ts, width), lambda h, t: (t, 2 * nblk + h)),
            pl.BlockSpec((ts, LANES), lambda h, t: (t, 0)),
            pl.BlockSpec((LANES, ts), lambda h, t: (0, t)),
            pl.BlockSpec((ts, width), lambda h, t: (t, sz0 + h)),
            pl.BlockSpec((1, HEAD_DIM), lambda h, t: (0, 0)),
        ],
        out_specs=pl.BlockSpec((ts, width), lambda h, t: (t, h)),
        out_shape=jax.ShapeDtypeStruct((s, n_heads * HEAD_DIM), BF16),
        scratch_shapes=[pltpu.VMEM((hg, HEAD_DIM, HEAD_DIM), F32)],
        compiler_params=_cparams("parallel", "arbitrary"),
        name="gated_delta_rule",
    )(qkvn, qkvn, qkvn, bd, bdt, qkvz, nw)


def _merge_kernel(c_ref, og_ref, wc_ref, wg_ref, bc_ref, ga_ref, gb_ref, m_ref, wcb_ref, wgb_ref):
    _cache_bf16([(wc_ref, wcb_ref), (wg_ref, wgb_ref)])
    for rows in _row_parts(c_ref.shape[0]):
        yc = _dot(c_ref[rows, :], wcb_ref[...]) + bc_ref[...]
        yg = _dot(og_ref[rows, :], wgb_ref[...])
        m_ref[rows, :] = (ga_ref[rows, :] * yc + gb_ref[rows, :] * yg).astype(m_ref.dtype)


def _merge_proj(c2, og, wc, wg, bc, gates, layer, *, tm, tn):
    m, k = c2.shape
    n = wc.shape[2]
    return pl.pallas_call(
        _merge_kernel,
        grid=(n // tn, m // tm),
        in_specs=[
            pl.BlockSpec((tm, k), lambda j, i: (i, 0)),
            pl.BlockSpec((tm, k), lambda j, i: (i, 0)),
            _wspec(k, tn, layer, 0), _wspec(k, tn, layer, 0), _bspec(tn, layer, 0),
            pl.BlockSpec((tm, tn), lambda j, i: (i, j)),
            pl.BlockSpec((tm, tn), lambda j, i: (i, n // tn + j)),
        ],
        out_specs=pl.BlockSpec((tm, tn), lambda j, i: (i, j)),
        out_shape=jax.ShapeDtypeStruct((m, n), BF16),
        scratch_shapes=[pltpu.VMEM((k, tn), BF16), pltpu.VMEM((k, tn), BF16)],
        compiler_params=_cparams("parallel", "arbitrary"),
        name="merge_proj",
    )(c2, og, wc, wg, bc, gates, gates)


def _proj_ln_kernel(a_ref, w_ref, res_ref, g_ref, b_ref, o_ref, ob_ref, acc_ref, *, alpha, nk):
    kk = pl.program_id(1)
    parts = _row_parts(a_ref.shape[0])

    def finish(partial_sum):
        for rows in parts:
            acc = partial_sum(rows) + _dot(a_ref[rows, :], w_ref[...])
            y = _layer_norm(acc, g_ref[...], b_ref[...])
            o_ref[rows, :] = y
            ob_ref[rows, :] = y.astype(ob_ref.dtype)

    if nk == 1:
        finish(lambda rows: alpha * res_ref[rows, :])
        return

    @pl.when(kk == 0)
    def _():
        acc_ref[...] = alpha * res_ref[...] + _dot(a_ref[...], w_ref[...])

    @pl.when((kk > 0) & (kk < nk - 1))
    def _():
        acc_ref[...] += _dot(a_ref[...], w_ref[...])

    @pl.when(kk == nk - 1)
    def _():
        finish(lambda rows: acc_ref[rows, :])


def _proj_ln(a, w, res, g, b, layer, *, alpha, tm, tk):
    m, k = a.shape
    n = w.shape[2]
    nk = k // tk
    w_mode = dict(pipeline_mode=pl.Buffered(1)) if nk == 1 else {}
    return pl.pallas_call(
        functools.partial(_proj_ln_kernel, alpha=alpha, nk=nk),
        grid=(m // tm, nk),
        in_specs=[
            pl.BlockSpec((tm, tk), lambda i, kk: (i, kk)),
            pl.BlockSpec((None, tk, n), lambda i, kk: (layer, kk, 0), **w_mode),
            pl.BlockSpec((tm, n), lambda i, kk: (i, 0)),
            pl.BlockSpec((None, 1, n), lambda i, kk: (layer, 0, 0)),
            pl.BlockSpec((None, 1, n), lambda i, kk: (layer, 0, 0)),
        ],
        out_specs=[
            pl.BlockSpec((tm, n), lambda i, kk: (i, 0)),
            pl.BlockSpec((tm, n), lambda i, kk: (i, 0)),
        ],
        out_shape=[jax.ShapeDtypeStruct((m, n), F32), jax.ShapeDtypeStruct((m, n), BF16)],
        scratch_shapes=[pltpu.VMEM((tm if nk > 1 else SUBLANES, n), F32)],
        compiler_params=_cparams("parallel", "arbitrary"),
        name="proj_ln",
    )(a, w, res, g, b)


def _swiglu_in_kernel(x_ref, wg_ref, wu_ref, o_ref, wgb_ref, wub_ref):
    _cache_bf16([(wg_ref, wgb_ref), (wu_ref, wub_ref)])
    for rows in _row_parts(x_ref.shape[0]):
        x = x_ref[rows, :]
        gate = _dot(x, wgb_ref[...])
        up = _dot(x, wub_ref[...])
        o_ref[rows, :] = (_silu(gate) * up).astype(o_ref.dtype)


def _swiglu_in(xb, w, layer, *, tm, tn):
    m, k = xb.shape
    d_ff = w.shape[2] // 2
    return pl.pallas_call(
        _swiglu_in_kernel,
        grid=(d_ff // tn, m // tm),
        in_specs=[
            pl.BlockSpec((tm, k), lambda j, i: (i, 0)),
            _wspec(k, tn, layer, 0), _wspec(k, tn, layer, d_ff // tn),
        ],
        out_specs=pl.BlockSpec((tm, tn), lambda j, i: (i, j)),
        out_shape=jax.ShapeDtypeStruct((m, d_ff), BF16),
        scratch_shapes=[pltpu.VMEM((k, tn), BF16), pltpu.VMEM((k, tn), BF16)],
        compiler_params=_cparams("parallel", "arbitrary"),
        name="swiglu_in",
    )(xb, w, w)


def _rows3(v):
    return v.reshape(v.shape[0], 1, v.shape[1]).astype(F32)


def _lane_row(v, offset):
    depth, n = v.shape
    return jnp.zeros((depth, 1, LANES), F32).at[:, 0, offset:offset + n].set(v.astype(F32))


TM = 1024
TN_ONE = 1024
TN_TWO = 512
TM_LN = 512
TS_CONV, RB_CONV, LN_ROWS_CONV = 256, 128, 64
GDN_HEADS, TS_GDN, GDN_UNIT = 8, 256, 128


def _layer(x, xb, p, layer, *, alpha):
    d_model = x.shape[1]
    n_heads = p["n_heads"]
    key_dim = n_heads * HEAD_DIM
    o_qkv = 2 * d_model
    o_z = o_qkv + 3 * key_dim
    o_beta = o_z + key_dim
    o_gate = o_beta + 2 * n_heads
    tm = min(TM, x.shape[0])
    tm_ln = min(TM_LN, x.shape[0])

    c = _glu_proj(xb, p["w_in"], p["b_in"], layer, n=d_model, tm=tm, tn=TN_TWO)
    c2 = _conv_ln(c, p["conv_dw_w"][layer], p["conv_dw_b"][layer], p["conv_ln_g"][layer],
                  p["conv_ln_b"][layer], ts=TS_CONV, rb=RB_CONV, ln_rows=LN_ROWS_CONV)

    qkvn = _qkv_proj(xb, p["w_in"], p["b_in"], p["short_conv_w"], layer, col0=o_qkv,
                     key_dim=key_dim, n=3 * key_dim, tm=tm, tn=TN_ONE)
    sz = _act_proj(xb, p["w_in"], p["b_in"], layer, col0=o_z, n=key_dim, act=_silu,
                   tm=tm, tn=TN_ONE, name="z_proj")
    gates = _shifted_proj(xb, p["w_in"], p["b_in"], layer, col0=o_gate, n=2 * d_model,
                          act=_sigmoid, tm=tm, tn=TN_ONE, name="gates_proj")
    bd, bdt = _beta_decay_proj(xb, p["w_in"], p["b_in"], p["a_log_row"], p["dt_row"], layer,
                               col0=o_beta, n_heads=n_heads, tm=tm_ln)
    og = _gdn(qkvn, bd, bdt, sz, p["gdn_norm_w"][layer], n_heads=n_heads, sz_col0=0,
              hg=GDN_HEADS, ts=TS_GDN, unit=GDN_UNIT)

    m = _merge_proj(c2, og, p["w_conv_proj"], p["w_gdn_proj"], p["b_conv_proj"], gates, layer,
                    tm=tm, tn=TN_TWO)
    x1, x1b = _proj_ln(m, p["w_out"], x, p["ln1_g"], p["ln1_b"], layer, alpha=alpha, tm=tm_ln,
                       tk=d_model)
    hff = _swiglu_in(x1b, p["w_ffn_in"], layer, tm=tm, tn=TN_TWO)
    x2, x2b = _proj_ln(hff, p["w_ffn_out"], x1, p["ln2_g"], p["ln2_b"], layer, alpha=alpha,
                       tm=tm_ln, tk=hff.shape[1] // 4)
    return x2, x2b


def kernel(x, w_in, b_in, conv_dw_w, conv_dw_b, conv_ln_g, conv_ln_b, w_conv_proj, b_conv_proj,
           short_conv_w, a_log, dt_bias, gdn_norm_w, w_gdn_proj, w_out, ln1_g, ln1_b,
           w_ffn_in, w_ffn_out, ln2_g, ln2_b):
    bsz, seq, d_model = x.shape
    depth, n_heads = a_log.shape
    alpha = (2.0 * depth) ** 0.25
    p = dict(
        n_heads=n_heads, w_in=jnp.swapaxes(w_in, 1, 2), b_in=_rows3(b_in),
        a_log_row=_lane_row(a_log, n_heads), dt_row=_lane_row(dt_bias, n_heads),
        conv_dw_w=conv_dw_w, conv_dw_b=_rows3(conv_dw_b), conv_ln_g=_rows3(conv_ln_g),
        conv_ln_b=_rows3(conv_ln_b), w_conv_proj=w_conv_proj, b_conv_proj=_rows3(b_conv_proj),
        short_conv_w=short_conv_w, gdn_norm_w=_rows3(gdn_norm_w),
        w_gdn_proj=w_gdn_proj, w_out=w_out.astype(BF16), ln1_g=_rows3(ln1_g), ln1_b=_rows3(ln1_b),
        w_ffn_in=w_ffn_in, w_ffn_out=w_ffn_out.astype(BF16), ln2_g=_rows3(ln2_g), ln2_b=_rows3(ln2_b))
    outs = []
    for bi in range(bsz):
        h = x[bi]
        hb = h.astype(BF16)
        for layer in range(depth):
            h, hb = _layer(h, hb, p, layer, alpha=alpha)
        outs.append(h)
    return jnp.stack(outs, axis=0)
```

```python
import functools

import jax
import jax.numpy as jnp
from jax import lax
from jax.experimental import pallas as pl
from jax.experimental.pallas import tpu as pltpu

F32 = jnp.float32
BF16 = jnp.bfloat16

HEAD_DIM = 128
CONV_WIDTH = 31
SHORT_CONV = 4
CHUNK = 64
INV_BLOCK = 16
LN_EPS = 1e-5
LANES = 128
SUBLANES = 8
VMEM_LIMIT = 48 * 1024 * 1024
CAST_ROWS = 256
ROW_PART = 256


def _cparams(*sem):
    return pltpu.CompilerParams(dimension_semantics=sem, vmem_limit_bytes=VMEM_LIMIT)


def _dot(a, b):
    return jnp.dot(a, b, preferred_element_type=F32)


def _dot_nt(a, b):
    return lax.dot_general(a, b, (((1,), (1,)), ((), ())), preferred_element_type=F32)


def _dot_tn(a, b):
    return lax.dot_general(a, b, (((0,), (0,)), ((), ())), preferred_element_type=F32)


def _bf(x):
    return x.astype(BF16)


def _sigmoid(x):
    return 0.5 * jnp.tanh(0.5 * x) + 0.5


def _silu(x):
    hx = 0.5 * x
    return hx * jnp.tanh(hx) + hx


def _layer_norm(y, g, b):
    mu = jnp.mean(y, axis=-1, keepdims=True)
    d = y - mu
    var = jnp.mean(d * d, axis=-1, keepdims=True)
    return d * lax.rsqrt(var + LN_EPS) * g + b


def _cache_bf16(pairs):
    @pl.when(pl.program_id(1) == 0)
    def _():
        for w_ref, wb_ref in pairs:
            def body(r, carry, w_ref=w_ref, wb_ref=wb_ref):
                r0 = pl.multiple_of(r * CAST_ROWS, CAST_ROWS)
                wb_ref[pl.ds(r0, CAST_ROWS), :] = w_ref[pl.ds(r0, CAST_ROWS), :].astype(BF16)
                return carry

            lax.fori_loop(0, w_ref.shape[0] // CAST_ROWS, body, 0)


def _wspec(k, tn, layer, col_tile0):
    return pl.BlockSpec((None, k, tn), lambda j, i: (layer, 0, col_tile0 + j))


def _wtspec(k, tn, layer, row_tile0):
    return pl.BlockSpec((None, tn, k), lambda j, i: (layer, row_tile0 + j, 0))


def _bspec(tn, layer, col_tile0):
    return pl.BlockSpec((None, 1, tn), lambda j, i: (layer, 0, col_tile0 + j))


def _glu_kernel(x_ref, wa_ref, wb_ref, ba_ref, bb_ref, o_ref, wab_ref, wbb_ref):
    _cache_bf16([(wa_ref, wab_ref), (wb_ref, wbb_ref)])
    for rows in _row_parts(x_ref.shape[0]):
        x = x_ref[rows, :]
        a = _dot_nt(x, wab_ref[...]) + ba_ref[...]
        b = _dot_nt(x, wbb_ref[...]) + bb_ref[...]
        o_ref[rows, :] = a * _sigmoid(b)


def _glu_proj(xb, wt, b, layer, *, n, tm, tn):
    m, k = xb.shape
    return pl.pallas_call(
        _glu_kernel,
        grid=(n // tn, m // tm),
        in_specs=[
            pl.BlockSpec((tm, k), lambda j, i: (i, 0)),
            _wtspec(k, tn, layer, 0), _wtspec(k, tn, layer, n // tn),
            _bspec(tn, layer, 0), _bspec(tn, layer, n // tn),
        ],
        out_specs=pl.BlockSpec((tm, tn), lambda j, i: (i, j)),
        out_shape=jax.ShapeDtypeStruct((m, n), F32),
        scratch_shapes=[pltpu.VMEM((tn, k), BF16), pltpu.VMEM((tn, k), BF16)],
        compiler_params=_cparams("parallel", "arbitrary"),
        name="glu_proj",
    )(xb, wt, wt, b, b)


def _row_parts(tm):
    return [slice(r, r + ROW_PART) for r in range(0, tm, ROW_PART)]


def _act_proj_kernel(x_ref, w_ref, b_ref, o_ref, wb_ref, *, act):
    _cache_bf16([(w_ref, wb_ref)])
    for rows in _row_parts(x_ref.shape[0]):
        acc = _dot_nt(x_ref[rows, :], wb_ref[...]) + b_ref[...]
        o_ref[rows, :] = act(acc).astype(o_ref.dtype)


def _shifted_proj_kernel(x_ref, w_ref, wn_ref, b_ref, bn_ref, o_ref, wb_ref, bs_ref, *, shift, act):
    tn = w_ref.shape[0]

    @pl.when(pl.program_id(1) == 0)
    def _():
        wb_ref[0:tn - shift, :] = w_ref[shift:tn, :].astype(BF16)
        wb_ref[tn - shift:tn, :] = wn_ref[...].astype(BF16)
        bs_ref[...] = jnp.concatenate([b_ref[...], bn_ref[...]], axis=1)[:, shift:shift + tn]

    for rows in _row_parts(x_ref.shape[0]):
        acc = _dot_nt(x_ref[rows, :], wb_ref[...]) + bs_ref[...]
        o_ref[rows, :] = act(acc).astype(o_ref.dtype)


def _shifted_proj(xb, wt, b, layer, *, col0, n, act, tm, tn, name):
    m, k = xb.shape
    shift = col0 % LANES
    t0 = (col0 - shift) // tn
    return pl.pallas_call(
        functools.partial(_shifted_proj_kernel, shift=shift, act=act),
        grid=(n // tn, m // tm),
        in_specs=[
            pl.BlockSpec((tm, k), lambda j, i: (i, 0)),
            _wtspec(k, tn, layer, t0),
            pl.BlockSpec((None, shift, k), lambda j, i: (layer, (t0 + j + 1) * (tn // shift), 0)),
            _bspec(tn, layer, t0),
            pl.BlockSpec((None, 1, LANES), lambda j, i: (layer, 0, (t0 + j + 1) * (tn // LANES))),
        ],
        out_specs=pl.BlockSpec((tm, tn), lambda j, i: (i, j)),
        out_shape=jax.ShapeDtypeStruct((m, n), BF16),
        scratch_shapes=[pltpu.VMEM((tn, k), BF16), pltpu.VMEM((1, tn), F32)],
        compiler_params=_cparams("parallel", "arbitrary"),
        name=name,
    )(xb, wt, wt, b, b)


def _qkv_kernel(x_ref, w_ref, b_ref, cw_ref, o_ref, wb_ref, tail_ref, *, n_q, n_qk, q_scale):
    _cache_bf16([(w_ref, wb_ref)])
    j = pl.program_id(0)
    tn = w_ref.shape[0]

    @pl.when(pl.program_id(1) == 0)
    def _():
        tail_ref[...] = jnp.zeros(tail_ref.shape, F32)

    is_qk = j < n_qk
    scale = jnp.where(j < n_q, q_scale, 1.0).astype(F32)
    prev = tail_ref[...]
    half_cw = 0.5 * cw_ref[...]
    parts = _row_parts(x_ref.shape[0])
    dots = [_dot_nt(x_ref[parts[0], :], wb_ref[...])]
    for p, rows in enumerate(parts):
        if p + 1 < len(parts):
            dots.append(_dot_nt(x_ref[parts[p + 1], :], wb_ref[...]))
        acc = dots[p] + b_ref[...]
        nr = acc.shape[0]
        full = jnp.concatenate([prev, acc], axis=0)
        prev = acc[nr - SUBLANES:]
        hy = acc * half_cw[SHORT_CONV - 1:SHORT_CONV, :]
        for t in range(SHORT_CONV - 1):
            shifted = pltpu.roll(full, SHORT_CONV - 1 - t, 0)[SUBLANES:]
            hy = hy + shifted * half_cw[t:t + 1, :]
        y = hy * jnp.tanh(hy) + hy
        for h in range(tn // HEAD_DIM):
            cols = slice(h * HEAD_DIM, (h + 1) * HEAD_DIM)
            yh = y[:, cols]
            inv = lax.rsqrt(jnp.sum(yh * yh, axis=-1, keepdims=True) + 1e-6) * scale
            o_ref[rows, cols] = (yh * jnp.where(is_qk, inv, 1.0)).astype(o_ref.dtype)
    tail_ref[...] = prev


def _qkv_proj(xb, wt, b, cw, layer, *, col0, key_dim, n, tm, tn):
    m, k = xb.shape
    return pl.pallas_call(
        functools.partial(_qkv_kernel, n_q=key_dim // tn, n_qk=2 * key_dim // tn,
                          q_scale=HEAD_DIM ** -0.5),
        grid=(n // tn, m // tm),
        in_specs=[
            pl.BlockSpec((tm, k), lambda j, i: (i, 0)),
            _wtspec(k, tn, layer, col0 // tn), _bspec(tn, layer, col0 // tn),
            pl.BlockSpec((None, SHORT_CONV, tn), lambda j, i: (layer, 0, j)),
        ],
        out_specs=pl.BlockSpec((tm, tn), lambda j, i: (i, j)),
        out_shape=jax.ShapeDtypeStruct((m, n), BF16),
        scratch_shapes=[pltpu.VMEM((tn, k), BF16), pltpu.VMEM((SUBLANES, tn), F32)],
        compiler_params=_cparams("parallel", "arbitrary"),
        name="qkv_proj",
    )(xb, wt, b, cw)


def _act_proj(xb, wt, b, layer, *, col0, n, act, tm, tn, name):
    m, k = xb.shape
    return pl.pallas_call(
        functools.partial(_act_proj_kernel, act=act),
        grid=(n // tn, m // tm),
        in_specs=[
            pl.BlockSpec((tm, k), lambda j, i: (i, 0)),
            _wtspec(k, tn, layer, col0 // tn), _bspec(tn, layer, col0 // tn),
        ],
        out_specs=pl.BlockSpec((tm, tn), lambda j, i: (i, j)),
        out_shape=jax.ShapeDtypeStruct((m, n), BF16),
        scratch_shapes=[pltpu.VMEM((tn, k), BF16)],
        compiler_params=_cparams("parallel", "arbitrary"),
        name=name,
    )(xb, wt, b)


def _beta_decay_kernel(x_ref, w_ref, b_ref, a_ref, dt_ref, o_ref, ot_ref, wb_ref, *, n_heads):
    @pl.when(pl.program_id(0) == 0)
    def _():
        wb_ref[...] = jnp.zeros(wb_ref.shape, BF16)
        wb_ref[0:2 * n_heads, :] = w_ref[...].astype(BF16)

    lane = lax.broadcasted_iota(jnp.int32, (x_ref.shape[0], LANES), 1)
    acc = _dot_nt(x_ref[...], wb_ref[...]) + jnp.where(lane < 2 * n_heads, b_ref[...], 0.0)
    beta = _sigmoid(acc)
    g = -jnp.exp(a_ref[...]) * jax.nn.softplus(acc + dt_ref[...])
    pos = lax.broadcasted_iota(jnp.int32, acc.shape, 0) % CHUNK
    shift = 1
    while shift < CHUNK:
        g = g + jnp.where(pos >= shift, pltpu.roll(g, shift, 0), 0.0)
        shift *= 2
    out = jnp.where(lane < n_heads, beta, g)
    o_ref[...] = out
    ot_ref[...] = out.T


def _beta_decay_proj(xb, wt, b, a_log_row, dt_row, layer, *, col0, n_heads, tm):
    m, k = xb.shape
    cblk = col0 // LANES
    return pl.pallas_call(
        functools.partial(_beta_decay_kernel, n_heads=n_heads),
        grid=(m // tm,),
        in_specs=[
            pl.BlockSpec((tm, k), lambda i: (i, 0)),
            pl.BlockSpec((None, 2 * n_heads, k), lambda i: (layer, col0 // (2 * n_heads), 0)),
            pl.BlockSpec((None, 1, LANES), lambda i: (layer, 0, cblk)),
            pl.BlockSpec((None, 1, LANES), lambda i: (layer, 0, 0)),
            pl.BlockSpec((None, 1, LANES), lambda i: (layer, 0, 0)),
        ],
        out_specs=[pl.BlockSpec((tm, LANES), lambda i: (i, 0)),
                   pl.BlockSpec((LANES, tm), lambda i: (0, i))],
        out_shape=[jax.ShapeDtypeStruct((m, LANES), F32), jax.ShapeDtypeStruct((LANES, m), F32)],
        scratch_shapes=[pltpu.VMEM((LANES, k), BF16)],
        compiler_params=_cparams("arbitrary"),
        name="beta_decay_proj",
    )(xb, wt, b, a_log_row, dt_row)


def _conv_ln_kernel(halo_ref, x_ref, w_ref, b_ref, g_ref, bn_ref, o_ref, xs_ref, y_ref, *,
                    ts, halo, rb, ln_rows):
    i = pl.program_id(0)
    ch = xs_ref.shape[1]

    @pl.when(i == 0)
    def _():
        xs_ref[0:halo, :] = jnp.zeros((halo, ch), F32)

    @pl.when(i > 0)
    def _():
        xs_ref[0:halo, :] = halo_ref[...]

    xs_ref[halo:halo + ts, :] = x_ref[...]
    n_rb = ts // rb

    def conv_block(idx, carry):
        c0 = pl.multiple_of((idx // n_rb) * LANES, LANES)
        r0 = (idx % n_rb) * rb
        lanes = pl.ds(c0, LANES)
        acc = jnp.zeros((rb, LANES), F32) + b_ref[:, lanes]
        for r in range(SUBLANES):
            lead = 0 if r == 0 else SUBLANES
            z = None
            for p in range((CONV_WIDTH - 1 - r) // SUBLANES + 1):
                j = CONV_WIDTH - 1 - (SUBLANES * p + r)
                row = pl.multiple_of(r0 + (halo - lead - SUBLANES * p), SUBLANES)
                term = xs_ref[pl.ds(row, rb + lead), lanes] * w_ref[j:j + 1, lanes]
                z = term if z is None else z + term
            acc = acc + (z if r == 0 else z[SUBLANES - r:SUBLANES - r + rb])
        y_ref[pl.ds(pl.multiple_of(r0, rb), rb), lanes] = acc
        return carry

    lax.fori_loop(0, n_rb * (ch // LANES), conv_block, 0)

    def ln_block(r, carry):
        r0 = pl.multiple_of(r * ln_rows, ln_rows)
        y = _layer_norm(y_ref[pl.ds(r0, ln_rows), :], g_ref[...], bn_ref[...])
        o_ref[pl.ds(r0, ln_rows), :] = _silu(y).astype(o_ref.dtype)
        return carry

    lax.fori_loop(0, ts // ln_rows, ln_block, 0)


def _conv_ln(c, w, b, g, bn, *, ts, rb, ln_rows):
    s, ch = c.shape
    halo = 4 * SUBLANES
    return pl.pallas_call(
        functools.partial(_conv_ln_kernel, ts=ts, halo=halo, rb=rb, ln_rows=ln_rows),
        grid=(s // ts,),
        in_specs=[
            pl.BlockSpec((halo, ch), lambda i: (jnp.maximum(i * (ts // halo) - 1, 0), 0)),
            pl.BlockSpec((ts, ch), lambda i: (i, 0)),
            pl.BlockSpec((CONV_WIDTH, ch), lambda i: (0, 0)),
            pl.BlockSpec((1, ch), lambda i: (0, 0)),
            pl.BlockSpec((1, ch), lambda i: (0, 0)),
            pl.BlockSpec((1, ch), lambda i: (0, 0)),
        ],
        out_specs=pl.BlockSpec((ts, ch), lambda i: (i, 0)),
        out_shape=jax.ShapeDtypeStruct((s, ch), BF16),
        scratch_shapes=[pltpu.VMEM((halo + ts, ch), F32), pltpu.VMEM((ts, ch), F32)],
        compiler_params=_cparams("parallel"),
        name="conv_ln",
    )(c, c, w, b, g, bn)


def _gdn_kernel(q_ref, k_ref, v_ref, bd_ref, gt_ref, sz_ref, nw_ref, o_ref, s_ref, *,
                hg, ts, unit, n_heads):
    c = CHUNK
    hgrp = pl.program_id(0)

    @pl.when(pl.program_id(1) == 0)
    def _():
        s_ref[...] = jnp.zeros(s_ref.shape, F32)

    bd = bd_ref[...]
    lane = lax.broadcasted_iota(jnp.int32, (ts, LANES), 1)
    ri = lax.broadcasted_iota(jnp.int32, (unit, unit), 0)
    ci = lax.broadcasted_iota(jnp.int32, (unit, unit), 1)
    same_chunk = (ri // c) == (ci // c)
    causal = same_chunk & (ri >= ci)
    strict = same_chunk & (ri > ci)
    inv_block = (ri // INV_BLOCK) == (ci // INV_BLOCK)
    eye = (ri == ci).astype(F32)
    row_chunk = lax.broadcasted_iota(jnp.int32, (unit, 1), 0) // c
    nw = nw_ref[...]
    n_units = ts // unit
    cpu = unit // c
    cat = jnp.concatenate

    st = []
    for hl in range(hg):
        head = hgrp * hg + hl
        cols = slice(hl * HEAD_DIM, (hl + 1) * HEAD_DIM)
        beta_col = jnp.sum(jnp.where(lane == head, bd, 0.0), axis=1, keepdims=True)
        g_col = jnp.sum(jnp.where(lane == n_heads + head, bd, 0.0), axis=1, keepdims=True)
        g_row = gt_ref[pl.ds(n_heads + head, 1), :]
        for un in range(n_units):
            rows = slice(un * unit, (un + 1) * unit)
            gc, gr, bc = g_col[rows], g_row[:, rows], beta_col[rows]
            decay = jnp.where(causal, jnp.exp(jnp.where(causal, gc - gr, 0.0)), 0.0)
            qbf = q_ref[rows, cols]
            kbf = k_ref[rows, cols]
            k = kbf.astype(F32)
            kb = k * bc
            eg = jnp.exp(gc)
            gram = _dot_nt(cat([_bf(kb), qbf], axis=0), kbf)
            low = jnp.where(strict, gram[:unit] * decay, 0.0)
            dg = jnp.where(inv_block, low, 0.0)
            g_last = [gr[:, cc * c + c - 1:cc * c + c] for cc in range(cpu)]
            g_last_col = g_last[0]
            for cc in range(1, cpu):
                g_last_col = jnp.where(row_chunk == cc, g_last[cc], g_last_col)
            kd = _bf(k * jnp.exp(g_last_col - gc))
            st.append(dict(
                hl=hl, row0=un * unit, attn=_bf(gram[unit:] * decay), dg=dg, n=_bf(low - dg),
                p=eye - dg, rhs=_bf(cat([v_ref[rows, cols].astype(F32) * bc, kb * eg], axis=1)),
                qd=qbf.astype(F32) * eg, gamma=[jnp.exp(gl) for gl in g_last],
                kd=cat([jnp.where(row_chunk == cc, kd, jnp.zeros_like(kd)) for cc in range(cpu)], axis=1)))

    for d in st:
        dgb = _bf(d["dg"])
        d["pw"] = _dot(dgb, dgb)
    for _ in range(2):
        for d in st:
            pwb = _bf(d["pw"])
            r = _dot(cat([pwb, _bf(d["p"])], axis=0), pwb)
            d["pw"], d["p"] = r[:unit], d["p"] + r[unit:]
    for d in st:
        d["p"] = d["p"] + _dot(_bf(d["p"]), _bf(d["pw"]))
    for d in st:
        r = _dot(_bf(d["p"]), cat([d["n"], d["rhs"]], axis=1))
        d["m"], d["y"] = _bf(r[:, :unit]), r[:, unit:]
    for d in st:
        r = _dot(d["m"], cat([d["m"], _bf(d["y"])], axis=1))
        d["m2"], d["z"] = _bf(r[:, :unit]), d["y"] - r[:, unit:]
    for d in st:
        d["uw"] = _bf(d["z"] + _dot(d["m2"], _bf(d["z"])))
    for d in st:
        au = _dot(d["attn"], d["uw"])
        d["o_loc"] = au[:, :HEAD_DIM]
        d["q_eff"] = d["qd"] - au[:, HEAD_DIM:]
        d["bm"] = _dot_tn(d["kd"], d["uw"])

    states = [s_ref[hl] for hl in range(hg)]
    for un in range(n_units):
        for cc in range(cpu):
            for d in st:
                if d["row0"] != un * unit:
                    continue
                hl = d["hl"]
                cols = slice(hl * HEAD_DIM, (hl + 1) * HEAD_DIM)
                bm = d["bm"][cc * HEAD_DIM:(cc + 1) * HEAD_DIM]
                mq = cat([_bf(bm[:, HEAD_DIM:]), _bf(d["q_eff"][cc * c:(cc + 1) * c])], axis=0)
                ms_qs = _dot(mq, _bf(states[hl]))
                o = ms_qs[HEAD_DIM:] + d["o_loc"][cc * c:(cc + 1) * c]
                states[hl] = d["gamma"][cc] * states[hl] - ms_qs[:HEAD_DIM] + bm[:, :HEAD_DIM]
                rows = slice(d["row0"] + cc * c, d["row0"] + (cc + 1) * c)
                o = o * lax.rsqrt(jnp.mean(o * o, axis=-1, keepdims=True) + 1e-6)
                o = o * nw * sz_ref[rows, cols].astype(F32)
                o_ref[rows, cols] = o.astype(o_ref.dtype)
    for hl in range(hg):
        s_ref[hl] = states[hl]


def _gdn(qkvn, bd, bdt, qkvz, nw, *, n_heads, sz_col0, hg, ts, unit):
    s = qkvn.shape[0]
    width = hg * HEAD_DIM
    nblk = n_heads // hg
    sz0 = sz_col0 // width
    return pl.pallas_call(
        functools.partial(_gdn_kernel, hg=hg, ts=ts, unit=unit, n_heads=n_heads),
        grid=(nblk, s // ts),
        in_specs=[
            pl.BlockSpec((ts, width), lambda h, t: (t, h)),
            pl.BlockSpec((ts, width), lambda h, t: (t, nblk + h)),
            pl.BlockSpec((ts, width), lambda h, t: (t, 2 * nblk + h)),
            pl.BlockSpec((ts, LANES), lambda h, t: (t, 0)),
            pl.BlockSpec((LANES, ts), lambda h, t: (0, t)),
            pl.BlockSpec((ts, width), lambda h, t: (t, sz0 + h)),
            pl.BlockSpec((1, HEAD_DIM), lambda h, t: (0, 0)),
        ],
        out_specs=pl.BlockSpec((ts, width), lambda h, t: (t, h)),
        out_shape=jax.ShapeDtypeStruct((s, n_heads * HEAD_DIM), BF16),
        scratch_shapes=[pltpu.VMEM((hg, HEAD_DIM, HEAD_DIM), F32)],
        compiler_params=_cparams("parallel", "arbitrary"),
        name="gated_delta_rule",
    )(qkvn, qkvn, qkvn, bd, bdt, qkvz, nw)


def _merge_kernel(c_ref, og_ref, wc_ref, wg_ref, bc_ref, ga_ref, gb_ref, m_ref, wcb_ref, wgb_ref):
    _cache_bf16([(wc_ref, wcb_ref), (wg_ref, wgb_ref)])
    for rows in _row_parts(c_ref.shape[0]):
        yc = _dot(c_ref[rows, :], wcb_ref[...]) + bc_ref[...]
        yg = _dot(og_ref[rows, :], wgb_ref[...])
        m_ref[rows, :] = (ga_ref[rows, :] * yc + gb_ref[rows, :] * yg).astype(m_ref.dtype)


def _merge_proj(c2, og, wc, wg, bc, gates, layer, *, tm, tn):
    m, k = c2.shape
    n = wc.shape[2]
    return pl.pallas_call(
        _merge_kernel,
        grid=(n // tn, m // tm),
        in_specs=[
            pl.BlockSpec((tm, k), lambda j, i: (i, 0)),
            pl.BlockSpec((tm, k), lambda j, i: (i, 0)),
            _wspec(k, tn, layer, 0), _wspec(k, tn, layer, 0), _bspec(tn, layer, 0),
            pl.BlockSpec((tm, tn), lambda j, i: (i, j)),
            pl.BlockSpec((tm, tn), lambda j, i: (i, n // tn + j)),
        ],
        out_specs=pl.BlockSpec((tm, tn), lambda j, i: (i, j)),
        out_shape=jax.ShapeDtypeStruct((m, n), BF16),
        scratch_shapes=[pltpu.VMEM((k, tn), BF16), pltpu.VMEM((k, tn), BF16)],
        compiler_params=_cparams("parallel", "arbitrary"),
        name="merge_proj",
    )(c2, og, wc, wg, bc, gates, gates)


def _proj_ln_kernel(a_ref, w_ref, res_ref, g_ref, b_ref, o_ref, ob_ref, acc_ref, *, alpha, nk):
    kk = pl.program_id(1)
    parts = _row_parts(a_ref.shape[0])

    def finish(partial_sum):
        for rows in parts:
            acc = partial_sum(rows) + _dot(a_ref[rows, :], w_ref[...])
            y = _layer_norm(acc, g_ref[...], b_ref[...])
            o_ref[rows, :] = y
            ob_ref[rows, :] = y.astype(ob_ref.dtype)

    if nk == 1:
        finish(lambda rows: alpha * res_ref[rows, :])
        return

    @pl.when(kk == 0)
    def _():
        acc_ref[...] = alpha * res_ref[...] + _dot(a_ref[...], w_ref[...])

    @pl.when((kk > 0) & (kk < nk - 1))
    def _():
        acc_ref[...] += _dot(a_ref[...], w_ref[...])

    @pl.when(kk == nk - 1)
    def _():
        finish(lambda rows: acc_ref[rows, :])


def _proj_ln(a, w, res, g, b, layer, *, alpha, tm, tk):
    m, k = a.shape
    n = w.shape[2]
    nk = k // tk
    w_mode = dict(pipeline_mode=pl.Buffered(1)) if nk == 1 else {}
    return pl.pallas_call(
        functools.partial(_proj_ln_kernel, alpha=alpha, nk=nk),
        grid=(m // tm, nk),
        in_specs=[
            pl.BlockSpec((tm, tk), lambda i, kk: (i, kk)),
            pl.BlockSpec((None, tk, n), lambda i, kk: (layer, kk, 0), **w_mode),
            pl.BlockSpec((tm, n), lambda i, kk: (i, 0)),
            pl.BlockSpec((None, 1, n), lambda i, kk: (layer, 0, 0)),
            pl.BlockSpec((None, 1, n), lambda i, kk: (layer, 0, 0)),
        ],
        out_specs=[
            pl.BlockSpec((tm, n), lambda i, kk: (i, 0)),
            pl.BlockSpec((tm, n), lambda i, kk: (i, 0)),
        ],
        out_shape=[jax.ShapeDtypeStruct((m, n), F32), jax.ShapeDtypeStruct((m, n), BF16)],
        scratch_shapes=[pltpu.VMEM((tm if nk > 1 else SUBLANES, n), F32)],
        compiler_params=_cparams("parallel", "arbitrary"),
        name="proj_ln",
    )(a, w, res, g, b)


def _swiglu_in_kernel(x_ref, wg_ref, wu_ref, o_ref, wgb_ref, wub_ref):
    _cache_bf16([(wg_ref, wgb_ref), (wu_ref, wub_ref)])
    for rows in _row_parts(x_ref.shape[0]):
        x = x_ref[rows, :]
        gate = _dot(x, wgb_ref[...])
        up = _dot(x, wub_ref[...])
        o_ref[rows, :] = (_silu(gate) * up).astype(o_ref.dtype)


def _swiglu_in(xb, w, layer, *, tm, tn):
    m, k = xb.shape
    d_ff = w.shape[2] // 2
    return pl.pallas_call(
        _swiglu_in_kernel,
        grid=(d_ff // tn, m // tm),
        in_specs=[
            pl.BlockSpec((tm, k), lambda j, i: (i, 0)),
            _wspec(k, tn, layer, 0), _wspec(k, tn, layer, d_ff // tn),
        ],
        out_specs=pl.BlockSpec((tm, tn), lambda j, i: (i, j)),
        out_shape=jax.ShapeDtypeStruct((m, d_ff), BF16),
        scratch_shapes=[pltpu.VMEM((k, tn), BF16), pltpu.VMEM((k, tn), BF16)],
        compiler_params=_cparams("parallel", "arbitrary"),
        name="swiglu_in",
    )(xb, w, w)


def _rows3(v):
    return v.reshape(v.shape[0], 1, v.shape[1]).astype(F32)


def _lane_row(v, offset):
    depth, n = v.shape
    return jnp.zeros((depth, 1, LANES), F32).at[:, 0, offset:offset + n].set(v.astype(F32))


TM = 1024
TN_ONE = 1024
TN_TWO = 512
TM_SWIGLU = 2048
TM_LN = 512
TS_CONV, RB_CONV, LN_ROWS_CONV = 512, 128, 64
GDN_HEADS, TS_GDN, GDN_UNIT = 8, 256, 128


def _layer(x, xb, p, layer, *, alpha):
    d_model = x.shape[1]
    n_heads = p["n_heads"]
    key_dim = n_heads * HEAD_DIM
    o_qkv = 2 * d_model
    o_z = o_qkv + 3 * key_dim
    o_beta = o_z + key_dim
    o_gate = o_beta + 2 * n_heads
    tm = min(TM, x.shape[0])
    tm_ln = min(TM_LN, x.shape[0])

    c = _glu_proj(xb, p["w_in"], p["b_in"], layer, n=d_model, tm=tm, tn=TN_TWO)
    c2 = _conv_ln(c, p["conv_dw_w"][layer], p["conv_dw_b"][layer], p["conv_ln_g"][layer],
                  p["conv_ln_b"][layer], ts=TS_CONV, rb=RB_CONV, ln_rows=LN_ROWS_CONV)

    qkvn = _qkv_proj(xb, p["w_in"], p["b_in"], p["short_conv_w"], layer, col0=o_qkv,
                     key_dim=key_dim, n=3 * key_dim, tm=tm, tn=TN_ONE)
    sz = _act_proj(xb, p["w_in"], p["b_in"], layer, col0=o_z, n=key_dim, act=_silu,
                   tm=tm, tn=TN_ONE, name="z_proj")
    gates = _shifted_proj(xb, p["w_in"], p["b_in"], layer, col0=o_gate, n=2 * d_model,
                          act=_sigmoid, tm=tm, tn=TN_ONE, name="gates_proj")
    bd, bdt = _beta_decay_proj(xb, p["w_in"], p["b_in"], p["a_log_row"], p["dt_row"], layer,
                               col0=o_beta, n_heads=n_heads, tm=tm)
    og = _gdn(qkvn, bd, bdt, sz, p["gdn_norm_w"][layer], n_heads=n_heads, sz_col0=0,
              hg=GDN_HEADS, ts=TS_GDN, unit=GDN_UNIT)

    m = _merge_proj(c2, og, p["w_conv_proj"], p["w_gdn_proj"], p["b_conv_proj"], gates, layer,
                    tm=tm, tn=TN_TWO)
    x1, x1b = _proj_ln(m, p["w_out"], x, p["ln1_g"], p["ln1_b"], layer, alpha=alpha, tm=tm_ln,
                       tk=d_model)
    hff = _swiglu_in(x1b, p["w_ffn_in"], layer, tm=min(TM_SWIGLU, x.shape[0]), tn=TN_TWO)
    x2, x2b = _proj_ln(hff, p["w_ffn_out"], x1, p["ln2_g"], p["ln2_b"], layer, alpha=alpha,
                       tm=tm_ln, tk=hff.shape[1] // 4)
    return x2, x2b


def kernel(x, w_in, b_in, conv_dw_w, conv_dw_b, conv_ln_g, conv_ln_b, w_conv_proj, b_conv_proj,
           short_conv_w, a_log, dt_bias, gdn_norm_w, w_gdn_proj, w_out, ln1_g, ln1_b,
           w_ffn_in, w_ffn_out, ln2_g, ln2_b):
    bsz, seq, d_model = x.shape
    depth, n_heads = a_log.shape
    alpha = (2.0 * depth) ** 0.25
    p = dict(
        n_heads=n_heads, w_in=jnp.swapaxes(w_in, 1, 2), b_in=_rows3(b_in),
        a_log_row=_lane_row(a_log, n_heads), dt_row=_lane_row(dt_bias, n_heads),
        conv_dw_w=conv_dw_w, conv_dw_b=_rows3(conv_dw_b), conv_ln_g=_rows3(conv_ln_g),
        conv_ln_b=_rows3(conv_ln_b), w_conv_proj=w_conv_proj, b_conv_proj=_rows3(b_conv_proj),
        short_conv_w=short_conv_w, gdn_norm_w=_rows3(gdn_norm_w),
        w_gdn_proj=w_gdn_proj, w_out=w_out.astype(BF16), ln1_g=_rows3(ln1_g), ln1_b=_rows3(ln1_b),
        w_ffn_in=w_ffn_in, w_ffn_out=w_ffn_out.astype(BF16), ln2_g=_rows3(ln2_g), ln2_b=_rows3(ln2_b))
    outs = []
    for bi in range(bsz):
        h = x[bi]
        hb = h.astype(BF16)
        for layer in range(depth):
            h, hb = _layer(h, hb, p, layer, alpha=alpha)
        outs.append(h)
    return jnp.stack(outs, axis=0)
```

```python
import functools

import jax
import jax.numpy as jnp
from jax import lax
from jax.experimental import pallas as pl
from jax.experimental.pallas import tpu as pltpu

F32 = jnp.float32
BF16 = jnp.bfloat16

HEAD_DIM = 128
CONV_WIDTH = 31
SHORT_CONV = 4
CHUNK = 64
INV_BLOCK = 16
LN_EPS = 1e-5
LANES = 128
SUBLANES = 8
VMEM_LIMIT = 48 * 1024 * 1024
CAST_ROWS = 256
ROW_PART = 256


def _cparams(*sem):
    return pltpu.CompilerParams(dimension_semantics=sem, vmem_limit_bytes=VMEM_LIMIT)


def _dot(a, b):
    return jnp.dot(a, b, preferred_element_type=F32)


def _dot_nt(a, b):
    return lax.dot_general(a, b, (((1,), (1,)), ((), ())), preferred_element_type=F32)


def _dot_tn(a, b):
    return lax.dot_general(a, b, (((0,), (0,)), ((), ())), preferred_element_type=F32)


def _bf(x):
    return x.astype(BF16)


def _sigmoid(x):
    return 0.5 * jnp.tanh(0.5 * x) + 0.5


def _silu(x):
    hx = 0.5 * x
    return hx * jnp.tanh(hx) + hx


def _layer_norm(y, g, b):
    mu = jnp.mean(y, axis=-1, keepdims=True)
    d = y - mu
    var = jnp.mean(d * d, axis=-1, keepdims=True)
    return d * lax.rsqrt(var + LN_EPS) * g + b


def _cache_bf16(pairs):
    @pl.when(pl.program_id(1) == 0)
    def _():
        for w_ref, wb_ref in pairs:
            def body(r, carry, w_ref=w_ref, wb_ref=wb_ref):
                r0 = pl.multiple_of(r * CAST_ROWS, CAST_ROWS)
                wb_ref[pl.ds(r0, CAST_ROWS), :] = w_ref[pl.ds(r0, CAST_ROWS), :].astype(BF16)
                return carry

            lax.fori_loop(0, w_ref.shape[0] // CAST_ROWS, body, 0)


def _wspec(k, tn, layer, col_tile0):
    return pl.BlockSpec((None, k, tn), lambda j, i: (layer, 0, col_tile0 + j))


def _wtspec(k, tn, layer, row_tile0):
    return pl.BlockSpec((None, tn, k), lambda j, i: (layer, row_tile0 + j, 0))


def _bspec(tn, layer, col_tile0):
    return pl.BlockSpec((None, 1, tn), lambda j, i: (layer, 0, col_tile0 + j))


def _glu_kernel(x_ref, wa_ref, wb_ref, ba_ref, bb_ref, o_ref, wab_ref, wbb_ref):
    _cache_bf16([(wa_ref, wab_ref), (wb_ref, wbb_ref)])
    for rows in _row_parts(x_ref.shape[0]):
        x = x_ref[rows, :]
        a = _dot_nt(x, wab_ref[...]) + ba_ref[...]
        b = _dot_nt(x, wbb_ref[...]) + bb_ref[...]
        o_ref[rows, :] = a * _sigmoid(b)


def _glu_proj(xb, wt, b, layer, *, n, tm, tn):
    m, k = xb.shape
    return pl.pallas_call(
        _glu_kernel,
        grid=(n // tn, m // tm),
        in_specs=[
            pl.BlockSpec((tm, k), lambda j, i: (i, 0)),
            _wtspec(k, tn, layer, 0), _wtspec(k, tn, layer, n // tn),
            _bspec(tn, layer, 0), _bspec(tn, layer, n // tn),
        ],
        out_specs=pl.BlockSpec((tm, tn), lambda j, i: (i, j)),
        out_shape=jax.ShapeDtypeStruct((m, n), F32),
        scratch_shapes=[pltpu.VMEM((tn, k), BF16), pltpu.VMEM((tn, k), BF16)],
        compiler_params=_cparams("parallel", "arbitrary"),
        name="glu_proj",
    )(xb, wt, wt, b, b)


def _row_parts(tm):
    return [slice(r, r + ROW_PART) for r in range(0, tm, ROW_PART)]


def _act_proj_kernel(x_ref, w_ref, b_ref, o_ref, wb_ref, *, act):
    _cache_bf16([(w_ref, wb_ref)])
    for rows in _row_parts(x_ref.shape[0]):
        acc = _dot_nt(x_ref[rows, :], wb_ref[...]) + b_ref[...]
        o_ref[rows, :] = act(acc).astype(o_ref.dtype)


def _shifted_proj_kernel(x_ref, w_ref, wn_ref, b_ref, bn_ref, o_ref, wb_ref, bs_ref, *, shift, act):
    tn = w_ref.shape[0]

    @pl.when(pl.program_id(1) == 0)
    def _():
        wb_ref[0:tn - shift, :] = w_ref[shift:tn, :].astype(BF16)
        wb_ref[tn - shift:tn, :] = wn_ref[...].astype(BF16)
        bs_ref[...] = jnp.concatenate([b_ref[...], bn_ref[...]], axis=1)[:, shift:shift + tn]

    for rows in _row_parts(x_ref.shape[0]):
        acc = _dot_nt(x_ref[rows, :], wb_ref[...]) + bs_ref[...]
        o_ref[rows, :] = act(acc).astype(o_ref.dtype)


def _shifted_proj(xb, wt, b, layer, *, col0, n, act, tm, tn, name):
    m, k = xb.shape
    shift = col0 % LANES
    t0 = (col0 - shift) // tn
    return pl.pallas_call(
        functools.partial(_shifted_proj_kernel, shift=shift, act=act),
        grid=(n // tn, m // tm),
        in_specs=[
            pl.BlockSpec((tm, k), lambda j, i: (i, 0)),
            _wtspec(k, tn, layer, t0),
            pl.BlockSpec((None, shift, k), lambda j, i: (layer, (t0 + j + 1) * (tn // shift), 0)),
            _bspec(tn, layer, t0),
            pl.BlockSpec((None, 1, LANES), lambda j, i: (layer, 0, (t0 + j + 1) * (tn // LANES))),
        ],
        out_specs=pl.BlockSpec((tm, tn), lambda j, i: (i, j)),
        out_shape=jax.ShapeDtypeStruct((m, n), BF16),
        scratch_shapes=[pltpu.VMEM((tn, k), BF16), pltpu.VMEM((1, tn), F32)],
        compiler_params=_cparams("parallel", "arbitrary"),
        name=name,
    )(xb, wt, wt, b, b)


def _qkv_kernel(x_ref, w_ref, b_ref, cw_ref, o_ref, wb_ref, tail_ref, *, n_q, n_qk, q_scale):
    _cache_bf16([(w_ref, wb_ref)])
    j = pl.program_id(0)
    tn = w_ref.shape[0]

    @pl.when(pl.program_id(1) == 0)
    def _():
        tail_ref[...] = jnp.zeros(tail_ref.shape, F32)

    is_qk = j < n_qk
    scale = jnp.where(j < n_q, q_scale, 1.0).astype(F32)
    prev = tail_ref[...]
    half_cw = 0.5 * cw_ref[...]
    parts = _row_parts(x_ref.shape[0])
    dots = [_dot_nt(x_ref[parts[0], :], wb_ref[...])]
    for p, rows in enumerate(parts):
        if p + 1 < len(parts):
            dots.append(_dot_nt(x_ref[parts[p + 1], :], wb_ref[...]))
        acc = dots[p] + b_ref[...]
        nr = acc.shape[0]
        full = jnp.concatenate([prev, acc], axis=0)
        prev = acc[nr - SUBLANES:]
        hy = acc * half_cw[SHORT_CONV - 1:SHORT_CONV, :]
        for t in range(SHORT_CONV - 1):
            shifted = pltpu.roll(full, SHORT_CONV - 1 - t, 0)[SUBLANES:]
            hy = hy + shifted * half_cw[t:t + 1, :]
        y = hy * jnp.tanh(hy) + hy
        for h in range(tn // HEAD_DIM):
            cols = slice(h * HEAD_DIM, (h + 1) * HEAD_DIM)
            yh = y[:, cols]
            inv = lax.rsqrt(jnp.sum(yh * yh, axis=-1, keepdims=True) + 1e-6) * scale
            o_ref[rows, cols] = (yh * jnp.where(is_qk, inv, 1.0)).astype(o_ref.dtype)
    tail_ref[...] = prev


def _qkv_proj(xb, wt, b, cw, layer, *, col0, key_dim, n, tm, tn):
    m, k = xb.shape
    return pl.pallas_call(
        functools.partial(_qkv_kernel, n_q=key_dim // tn, n_qk=2 * key_dim // tn,
                          q_scale=HEAD_DIM ** -0.5),
        grid=(n // tn, m // tm),
        in_specs=[
            pl.BlockSpec((tm, k), lambda j, i: (i, 0)),
            _wtspec(k, tn, layer, col0 // tn), _bspec(tn, layer, col0 // tn),
            pl.BlockSpec((None, SHORT_CONV, tn), lambda j, i: (layer, 0, j)),
        ],
        out_specs=pl.BlockSpec((tm, tn), lambda j, i: (i, j)),
        out_shape=jax.ShapeDtypeStruct((m, n), BF16),
        scratch_shapes=[pltpu.VMEM((tn, k), BF16), pltpu.VMEM((SUBLANES, tn), F32)],
        compiler_params=_cparams("parallel", "arbitrary"),
        name="qkv_proj",
    )(xb, wt, b, cw)


def _act_proj(xb, wt, b, layer, *, col0, n, act, tm, tn, name):
    m, k = xb.shape
    return pl.pallas_call(
        functools.partial(_act_proj_kernel, act=act),
        grid=(n // tn, m // tm),
        in_specs=[
            pl.BlockSpec((tm, k), lambda j, i: (i, 0)),
            _wtspec(k, tn, layer, col0 // tn), _bspec(tn, layer, col0 // tn),
        ],
        out_specs=pl.BlockSpec((tm, tn), lambda j, i: (i, j)),
        out_shape=jax.ShapeDtypeStruct((m, n), BF16),
        scratch_shapes=[pltpu.VMEM((tn, k), BF16)],
        compiler_params=_cparams("parallel", "arbitrary"),
        name=name,
    )(xb, wt, b)


def _beta_decay_kernel(x_ref, w_ref, b_ref, a_ref, dt_ref, o_ref, ot_ref, wb_ref, *, n_heads):
    @pl.when(pl.program_id(0) == 0)
    def _():
        wb_ref[...] = jnp.zeros(wb_ref.shape, BF16)
        wb_ref[0:2 * n_heads, :] = w_ref[...].astype(BF16)

    lane = lax.broadcasted_iota(jnp.int32, (x_ref.shape[0], LANES), 1)
    acc = _dot_nt(x_ref[...], wb_ref[...]) + jnp.where(lane < 2 * n_heads, b_ref[...], 0.0)
    beta = _sigmoid(acc)
    g = -jnp.exp(a_ref[...]) * jax.nn.softplus(acc + dt_ref[...])
    pos = lax.broadcasted_iota(jnp.int32, acc.shape, 0) % CHUNK
    shift = 1
    while shift < CHUNK:
        g = g + jnp.where(pos >= shift, pltpu.roll(g, shift, 0), 0.0)
        shift *= 2
    out = jnp.where(lane < n_heads, beta, g)
    o_ref[...] = out
    ot_ref[...] = out.T


def _beta_decay_proj(xb, wt, b, a_log_row, dt_row, layer, *, col0, n_heads, tm):
    m, k = xb.shape
    cblk = col0 // LANES
    return pl.pallas_call(
        functools.partial(_beta_decay_kernel, n_heads=n_heads),
        grid=(m // tm,),
        in_specs=[
            pl.BlockSpec((tm, k), lambda i: (i, 0)),
            pl.BlockSpec((None, 2 * n_heads, k), lambda i: (layer, col0 // (2 * n_heads), 0)),
            pl.BlockSpec((None, 1, LANES), lambda i: (layer, 0, cblk)),
            pl.BlockSpec((None, 1, LANES), lambda i: (layer, 0, 0)),
            pl.BlockSpec((None, 1, LANES), lambda i: (layer, 0, 0)),
        ],
        out_specs=[pl.BlockSpec((tm, LANES), lambda i: (i, 0)),
                   pl.BlockSpec((LANES, tm), lambda i: (0, i))],
        out_shape=[jax.ShapeDtypeStruct((m, LANES), F32), jax.ShapeDtypeStruct((LANES, m), F32)],
        scratch_shapes=[pltpu.VMEM((LANES, k), BF16)],
        compiler_params=_cparams("arbitrary"),
        name="beta_decay_proj",
    )(xb, wt, b, a_log_row, dt_row)


def _conv_ln_kernel(halo_ref, x_ref, w_ref, b_ref, g_ref, bn_ref, o_ref, xs_ref, y_ref, *,
                    ts, halo, rb, ln_rows):
    i = pl.program_id(0)
    ch = xs_ref.shape[1]

    @pl.when(i == 0)
    def _():
        xs_ref[0:halo, :] = jnp.zeros((halo, ch), F32)

    @pl.when(i > 0)
    def _():
        xs_ref[0:halo, :] = halo_ref[...]

    xs_ref[halo:halo + ts, :] = x_ref[...]
    n_rb = ts // rb

    def conv_block(idx, carry):
        c0 = pl.multiple_of((idx // n_rb) * LANES, LANES)
        r0 = (idx % n_rb) * rb
        lanes = pl.ds(c0, LANES)
        acc = jnp.zeros((rb, LANES), F32) + b_ref[:, lanes]
        for r in range(SUBLANES):
            lead = 0 if r == 0 else SUBLANES
            z = None
            for p in range((CONV_WIDTH - 1 - r) // SUBLANES + 1):
                j = CONV_WIDTH - 1 - (SUBLANES * p + r)
                row = pl.multiple_of(r0 + (halo - lead - SUBLANES * p), SUBLANES)
                term = xs_ref[pl.ds(row, rb + lead), lanes] * w_ref[j:j + 1, lanes]
                z = term if z is None else z + term
            acc = acc + (z if r == 0 else z[SUBLANES - r:SUBLANES - r + rb])
        y_ref[pl.ds(pl.multiple_of(r0, rb), rb), lanes] = acc
        return carry

    lax.fori_loop(0, n_rb * (ch // LANES), conv_block, 0)

    def ln_block(r, carry):
        r0 = pl.multiple_of(r * ln_rows, ln_rows)
        y = _layer_norm(y_ref[pl.ds(r0, ln_rows), :], g_ref[...], bn_ref[...])
        o_ref[pl.ds(r0, ln_rows), :] = _silu(y).astype(o_ref.dtype)
        return carry

    lax.fori_loop(0, ts // ln_rows, ln_block, 0)


def _conv_ln(c, w, b, g, bn, *, ts, rb, ln_rows):
    s, ch = c.shape
    halo = 4 * SUBLANES
    return pl.pallas_call(
        functools.partial(_conv_ln_kernel, ts=ts, halo=halo, rb=rb, ln_rows=ln_rows),
        grid=(s // ts,),
        in_specs=[
            pl.BlockSpec((halo, ch), lambda i: (jnp.maximum(i * (ts // halo) - 1, 0), 0)),
            pl.BlockSpec((ts, ch), lambda i: (i, 0)),
            pl.BlockSpec((CONV_WIDTH, ch), lambda i: (0, 0)),
            pl.BlockSpec((1, ch), lambda i: (0, 0)),
            pl.BlockSpec((1, ch), lambda i: (0, 0)),
            pl.BlockSpec((1, ch), lambda i: (0, 0)),
        ],
        out_specs=pl.BlockSpec((ts, ch), lambda i: (i, 0)),
        out_shape=jax.ShapeDtypeStruct((s, ch), BF16),
        scratch_shapes=[pltpu.VMEM((halo + ts, ch), F32), pltpu.VMEM((ts, ch), F32)],
        compiler_params=_cparams("parallel"),
        name="conv_ln",
    )(c, c, w, b, g, bn)


def _gdn_kernel(q_ref, k_ref, v_ref, bd_ref, gt_ref, sz_ref, nw_ref, o_ref, s_ref, *,
                hg, ts, unit, n_heads):
    c = CHUNK
    hgrp = pl.program_id(0)

    @pl.when(pl.program_id(1) == 0)
    def _():
        s_ref[...] = jnp.zeros(s_ref.shape, F32)

    bd = bd_ref[...]
    lane = lax.broadcasted_iota(jnp.int32, (ts, LANES), 1)
    ri = lax.broadcasted_iota(jnp.int32, (unit, unit), 0)
    ci = lax.broadcasted_iota(jnp.int32, (unit, unit), 1)
    same_chunk = (ri // c) == (ci // c)
    causal = same_chunk & (ri >= ci)
    strict = same_chunk & (ri > ci)
    inv_block = (ri // INV_BLOCK) == (ci // INV_BLOCK)
    eye = (ri == ci).astype(F32)
    row_chunk = lax.broadcasted_iota(jnp.int32, (unit, 1), 0) // c
    nw = nw_ref[...]
    n_units = ts // unit
    cpu = unit // c
    cat = jnp.concatenate

    st = []
    for hl in range(hg):
        head = hgrp * hg + hl
        cols = slice(hl * HEAD_DIM, (hl + 1) * HEAD_DIM)
        beta_col = jnp.sum(jnp.where(lane == head, bd, 0.0), axis=1, keepdims=True)
        g_col = jnp.sum(jnp.where(lane == n_heads + head, bd, 0.0), axis=1, keepdims=True)
        g_row = gt_ref[pl.ds(n_heads + head, 1), :]
        for un in range(n_units):
            rows = slice(un * unit, (un + 1) * unit)
            gc, gr, bc = g_col[rows], g_row[:, rows], beta_col[rows]
            decay = jnp.where(causal, jnp.exp(jnp.where(causal, gc - gr, 0.0)), 0.0)
            qbf = q_ref[rows, cols]
            kbf = k_ref[rows, cols]
            k = kbf.astype(F32)
            kb = k * bc
            eg = jnp.exp(gc)
            gram = _dot_nt(cat([_bf(kb), qbf], axis=0), kbf)
            low = jnp.where(strict, gram[:unit] * decay, 0.0)
            dg = jnp.where(inv_block, low, 0.0)
            g_last = [gr[:, cc * c + c - 1:cc * c + c] for cc in range(cpu)]
            g_last_col = g_last[0]
            for cc in range(1, cpu):
                g_last_col = jnp.where(row_chunk == cc, g_last[cc], g_last_col)
            kd = _bf(k * jnp.exp(g_last_col - gc))
            st.append(dict(
                hl=hl, row0=un * unit, attn=_bf(gram[unit:] * decay), dg=dg, n=_bf(low - dg),
                p=eye - dg, rhs=_bf(cat([v_ref[rows, cols].astype(F32) * bc, kb * eg], axis=1)),
                qd=qbf.astype(F32) * eg, gamma=[jnp.exp(gl) for gl in g_last],
                kd=cat([jnp.where(row_chunk == cc, kd, jnp.zeros_like(kd)) for cc in range(cpu)], axis=1)))

    for d in st:
        dgb = _bf(d["dg"])
        d["pw"] = _dot(dgb, dgb)
    for _ in range(2):
        for d in st:
            pwb = _bf(d["pw"])
            r = _dot(cat([pwb, _bf(d["p"])], axis=0), pwb)
            d["pw"], d["p"] = r[:unit], d["p"] + r[unit:]
    for d in st:
        d["p"] = d["p"] + _dot(_bf(d["p"]), _bf(d["pw"]))
    for d in st:
        r = _dot(_bf(d["p"]), cat([d["n"], d["rhs"]], axis=1))
        d["m"], d["y"] = _bf(r[:, :unit]), r[:, unit:]
    for d in st:
        r = _dot(d["m"], cat([d["m"], _bf(d["y"])], axis=1))
        d["m2"], d["z"] = _bf(r[:, :unit]), d["y"] - r[:, unit:]
    for d in st:
        d["uw"] = _bf(d["z"] + _dot(d["m2"], _bf(d["z"])))
    for d in st:
        au = _dot(d["attn"], d["uw"])
        d["o_loc"] = au[:, :HEAD_DIM]
        d["q_eff"] = d["qd"] - au[:, HEAD_DIM:]
        d["bm"] = _dot_tn(d["kd"], d["uw"])

    states = [s_ref[hl] for hl in range(hg)]
    for un in range(n_units):
        for cc in range(cpu):
            for d in st:
                if d["row0"] != un * unit:
                    continue
                hl = d["hl"]
                cols = slice(hl * HEAD_DIM, (hl + 1) * HEAD_DIM)
                bm = d["bm"][cc * HEAD_DIM:(cc + 1) * HEAD_DIM]
                mq = cat([_bf(bm[:, HEAD_DIM:]), _bf(d["q_eff"][cc * c:(cc + 1) * c])], axis=0)
                ms_qs = _dot(mq, _bf(states[hl]))
                o = ms_qs[HEAD_DIM:] + d["o_loc"][cc * c:(cc + 1) * c]
                states[hl] = d["gamma"][cc] * states[hl] - ms_qs[:HEAD_DIM] + bm[:, :HEAD_DIM]
                rows = slice(d["row0"] + cc * c, d["row0"] + (cc + 1) * c)
                o = o * lax.rsqrt(jnp.mean(o * o, axis=-1, keepdims=True) + 1e-6)
                o = o * nw * sz_ref[rows, cols].astype(F32)
                o_ref[rows, cols] = o.astype(o_ref.dtype)
    for hl in range(hg):
        s_ref[hl] = states[hl]


def _gdn(qkvn, bd, bdt, qkvz, nw, *, n_heads, sz_col0, hg, ts, unit):
    s = qkvn.shape[0]
    width = hg * HEAD_DIM
    nblk = n_heads // hg
    sz0 = sz_col0 // width
    return pl.pallas_call(
        functools.partial(_gdn_kernel, hg=hg, ts=ts, unit=unit, n_heads=n_heads),
        grid=(nblk, s // ts),
        in_specs=[
            pl.BlockSpec((ts, width), lambda h, t: (t, h)),
            pl.BlockSpec((ts, width), lambda h, t: (t, nblk + h)),
            pl.BlockSpec((ts, width), lambda h, t: (t, 2 * nblk + h)),
            pl.BlockSpec((ts, LANES), lambda h, t: (t, 0)),
            pl.BlockSpec((LANES, ts), lambda h, t: (0, t)),
            pl.BlockSpec((ts, width), lambda h, t: (t, sz0 + h)),
            pl.BlockSpec((1, HEAD_DIM), lambda h, t: (0, 0)),
        ],
        out_specs=pl.BlockSpec((ts, width), lambda h, t: (t, h)),
        out_shape=jax.ShapeDtypeStruct((s, n_heads * HEAD_DIM), BF16),
        scratch_shapes=[pltpu.VMEM((hg, HEAD_DIM, HEAD_DIM), F32)],
        compiler_params=_cparams("parallel", "arbitrary"),
        name="gated_delta_rule",
    )(qkvn, qkvn, qkvn, bd, bdt, qkvz, nw)


def _merge_kernel(c_ref, og_ref, wc_ref, wg_ref, bc_ref, ga_ref, gb_ref, m_ref, wcb_ref, wgb_ref):
    _cache_bf16([(wc_ref, wcb_ref), (wg_ref, wgb_ref)])
    for rows in _row_parts(c_ref.shape[0]):
        yc = _dot(c_ref[rows, :], wcb_ref[...]) + bc_ref[...]
        yg = _dot(og_ref[rows, :], wgb_ref[...])
        m_ref[rows, :] = (ga_ref[rows, :] * yc + gb_ref[rows, :] * yg).astype(m_ref.dtype)


def _merge_proj(c2, og, wc, wg, bc, gates, layer, *, tm, tn):
    m, k = c2.shape
    n = wc.shape[2]
    return pl.pallas_call(
        _merge_kernel,
        grid=(n // tn, m // tm),
        in_specs=[
            pl.BlockSpec((tm, k), lambda j, i: (i, 0)),
            pl.BlockSpec((tm, k), lambda j, i: (i, 0)),
            _wspec(k, tn, layer, 0), _wspec(k, tn, layer, 0), _bspec(tn, layer, 0),
            pl.BlockSpec((tm, tn), lambda j, i: (i, j)),
            pl.BlockSpec((tm, tn), lambda j, i: (i, n // tn + j)),
        ],
        out_specs=pl.BlockSpec((tm, tn), lambda j, i: (i, j)),
        out_shape=jax.ShapeDtypeStruct((m, n), BF16),
        scratch_shapes=[pltpu.VMEM((k, tn), BF16), pltpu.VMEM((k, tn), BF16)],
        compiler_params=_cparams("parallel", "arbitrary"),
        name="merge_proj",
    )(c2, og, wc, wg, bc, gates, gates)


def _proj_ln_kernel(a_ref, w_ref, res_ref, g_ref, b_ref, o_ref, ob_ref, acc_ref, *, alpha, nk):
    kk = pl.program_id(1)
    parts = _row_parts(a_ref.shape[0])

    def finish(partial_sum):
        for rows in parts:
            acc = partial_sum(rows) + _dot(a_ref[rows, :], w_ref[...])
            y = _layer_norm(acc, g_ref[...], b_ref[...])
            o_ref[rows, :] = y
            ob_ref[rows, :] = y.astype(ob_ref.dtype)

    if nk == 1:
        finish(lambda rows: alpha * res_ref[rows, :])
        return

    @pl.when(kk == 0)
    def _():
        acc_ref[...] = alpha * res_ref[...] + _dot(a_ref[...], w_ref[...])

    @pl.when((kk > 0) & (kk < nk - 1))
    def _():
        acc_ref[...] += _dot(a_ref[...], w_ref[...])

    @pl.when(kk == nk - 1)
    def _():
        finish(lambda rows: acc_ref[rows, :])


def _proj_ln(a, w, res, g, b, layer, *, alpha, tm, tk):
    m, k = a.shape
    n = w.shape[2]
    nk = k // tk
    w_mode = dict(pipeline_mode=pl.Buffered(1)) if nk == 1 else {}
    return pl.pallas_call(
        functools.partial(_proj_ln_kernel, alpha=alpha, nk=nk),
        grid=(m // tm, nk),
        in_specs=[
            pl.BlockSpec((tm, tk), lambda i, kk: (i, kk)),
            pl.BlockSpec((None, tk, n), lambda i, kk: (layer, kk, 0), **w_mode),
            pl.BlockSpec((tm, n), lambda i, kk: (i, 0)),
            pl.BlockSpec((None, 1, n), lambda i, kk: (layer, 0, 0)),
            pl.BlockSpec((None, 1, n), lambda i, kk: (layer, 0, 0)),
        ],
        out_specs=[
            pl.BlockSpec((tm, n), lambda i, kk: (i, 0)),
            pl.BlockSpec((tm, n), lambda i, kk: (i, 0)),
        ],
        out_shape=[jax.ShapeDtypeStruct((m, n), F32), jax.ShapeDtypeStruct((m, n), BF16)],
        scratch_shapes=[pltpu.VMEM((tm if nk > 1 else SUBLANES, n), F32)],
        compiler_params=_cparams("parallel", "arbitrary"),
        name="proj_ln",
    )(a, w, res, g, b)


def _swiglu_in_kernel(x_ref, wg_ref, wu_ref, o_ref, wgb_ref, wub_ref):
    _cache_bf16([(wg_ref, wgb_ref), (wu_ref, wub_ref)])
    for rows in _row_parts(x_ref.shape[0]):
        x = x_ref[rows, :]
        gate = _dot(x, wgb_ref[...])
        up = _dot(x, wub_ref[...])
        o_ref[rows, :] = (_silu(gate) * up).astype(o_ref.dtype)


def _swiglu_in(xb, w, layer, *, tm, tn):
    m, k = xb.shape
    d_ff = w.shape[2] // 2
    return pl.pallas_call(
        _swiglu_in_kernel,
        grid=(d_ff // tn, m // tm),
        in_specs=[
            pl.BlockSpec((tm, k), lambda j, i: (i, 0)),
            _wspec(k, tn, layer, 0), _wspec(k, tn, layer, d_ff // tn),
        ],
        out_specs=pl.BlockSpec((tm, tn), lambda j, i: (i, j)),
        out_shape=jax.ShapeDtypeStruct((m, d_ff), BF16),
        scratch_shapes=[pltpu.VMEM((k, tn), BF16), pltpu.VMEM((k, tn), BF16)],
        compiler_params=_cparams("parallel", "arbitrary"),
        name="swiglu_in",
    )(xb, w, w)


def _rows3(v):
    return v.reshape(v.shape[0], 1, v.shape[1]).astype(F32)


def _lane_row(v, offset):
    depth, n = v.shape
    return jnp.zeros((depth, 1, LANES), F32).at[:, 0, offset:offset + n].set(v.astype(F32))


TM = 1024
TN_ONE = 1024
TN_TWO = 512
TM_SWIGLU = 2048
TM_LN = 512
TS_CONV, RB_CONV, LN_ROWS_CONV = 512, 256, 128
GDN_HEADS, TS_GDN, GDN_UNIT = 8, 256, 128


def _layer(x, xb, p, layer, *, alpha):
    d_model = x.shape[1]
    n_heads = p["n_heads"]
    key_dim = n_heads * HEAD_DIM
    o_qkv = 2 * d_model
    o_z = o_qkv + 3 * key_dim
    o_beta = o_z + key_dim
    o_gate = o_beta + 2 * n_heads
    tm = min(TM, x.shape[0])
    tm_ln = min(TM_LN, x.shape[0])

    c = _glu_proj(xb, p["w_in"], p["b_in"], layer, n=d_model, tm=tm, tn=TN_TWO)
    c2 = _conv_ln(c, p["conv_dw_w"][layer], p["conv_dw_b"][layer], p["conv_ln_g"][layer],
                  p["conv_ln_b"][layer], ts=TS_CONV, rb=RB_CONV, ln_rows=LN_ROWS_CONV)

    qkvn = _qkv_proj(xb, p["w_in"], p["b_in"], p["short_conv_w"], layer, col0=o_qkv,
                     key_dim=key_dim, n=3 * key_dim, tm=tm, tn=TN_ONE)
    sz = _act_proj(xb, p["w_in"], p["b_in"], layer, col0=o_z, n=key_dim, act=_silu,
                   tm=tm, tn=TN_ONE, name="z_proj")
    gates = _shifted_proj(xb, p["w_in"], p["b_in"], layer, col0=o_gate, n=2 * d_model,
                          act=_sigmoid, tm=tm, tn=TN_ONE, name="gates_proj")
    bd, bdt = _beta_decay_proj(xb, p["w_in"], p["b_in"], p["a_log_row"], p["dt_row"], layer,
                               col0=o_beta, n_heads=n_heads, tm=tm)
    og = _gdn(qkvn, bd, bdt, sz, p["gdn_norm_w"][layer], n_heads=n_heads, sz_col0=0,
              hg=GDN_HEADS, ts=TS_GDN, unit=GDN_UNIT)

    m = _merge_proj(c2, og, p["w_conv_proj"], p["w_gdn_proj"], p["b_conv_proj"], gates, layer,
                    tm=tm, tn=TN_TWO)
    x1, x1b = _proj_ln(m, p["w_out"], x, p["ln1_g"], p["ln1_b"], layer, alpha=alpha, tm=tm_ln,
                       tk=d_model)
    hff = _swiglu_in(x1b, p["w_ffn_in"], layer, tm=min(TM_SWIGLU, x.shape[0]), tn=TN_TWO)
    x2, x2b = _proj_ln(hff, p["w_ffn_out"], x1, p["ln2_g"], p["ln2_b"], layer, alpha=alpha,
                       tm=tm_ln, tk=hff.shape[1] // 4)
    return x2, x2b


def kernel(x, w_in, b_in, conv_dw_w, conv_dw_b, conv_ln_g, conv_ln_b, w_conv_proj, b_conv_proj,
           short_conv_w, a_log, dt_bias, gdn_norm_w, w_gdn_proj, w_out, ln1_g, ln1_b,
           w_ffn_in, w_ffn_out, ln2_g, ln2_b):
    bsz, seq, d_model = x.shape
    depth, n_heads = a_log.shape
    alpha = (2.0 * depth) ** 0.25
    p = dict(
        n_heads=n_heads, w_in=jnp.swapaxes(w_in, 1, 2), b_in=_rows3(b_in),
        a_log_row=_lane_row(a_log, n_heads), dt_row=_lane_row(dt_bias, n_heads),
        conv_dw_w=conv_dw_w, conv_dw_b=_rows3(conv_dw_b), conv_ln_g=_rows3(conv_ln_g),
        conv_ln_b=_rows3(conv_ln_b), w_conv_proj=w_conv_proj, b_conv_proj=_rows3(b_conv_proj),
        short_conv_w=short_conv_w, gdn_norm_w=_rows3(gdn_norm_w),
        w_gdn_proj=w_gdn_proj, w_out=w_out.astype(BF16), ln1_g=_rows3(ln1_g), ln1_b=_rows3(ln1_b),
        w_ffn_in=w_ffn_in, w_ffn_out=w_ffn_out.astype(BF16), ln2_g=_rows3(ln2_g), ln2_b=_rows3(ln2_b))
    outs = []
    for bi in range(bsz):
        h = x[bi]
        hb = h.astype(BF16)
        for layer in range(depth):
            h, hb = _layer(h, hb, p, layer, alpha=alpha)
        outs.append(h)
    return jnp.stack(outs, axis=0)
```

```python
import functools

import jax
import jax.numpy as jnp
from jax import lax
from jax.experimental import pallas as pl
from jax.experimental.pallas import tpu as pltpu

F32 = jnp.float32
BF16 = jnp.bfloat16

HEAD_DIM = 128
CONV_WIDTH = 31
SHORT_CONV = 4
CHUNK = 64
INV_BLOCK = 16
LN_EPS = 1e-5
LANES = 128
SUBLANES = 8
VMEM_LIMIT = 48 * 1024 * 1024
CAST_ROWS = 256
ROW_PART = 256


def _cparams(*sem):
    return pltpu.CompilerParams(dimension_semantics=sem, vmem_limit_bytes=VMEM_LIMIT)


def _dot(a, b):
    return jnp.dot(a, b, preferred_element_type=F32)


def _dot_nt(a, b):
    return lax.dot_general(a, b, (((1,), (1,)), ((), ())), preferred_element_type=F32)


def _dot_tn(a, b):
    return lax.dot_general(a, b, (((0,), (0,)), ((), ())), preferred_element_type=F32)


def _bf(x):
    return x.astype(BF16)


def _sigmoid(x):
    return 0.5 * jnp.tanh(0.5 * x) + 0.5


def _silu(x):
    hx = 0.5 * x
    return hx * jnp.tanh(hx) + hx


def _layer_norm(y, g, b):
    mu = jnp.mean(y, axis=-1, keepdims=True)
    d = y - mu
    var = jnp.mean(d * d, axis=-1, keepdims=True)
    return d * lax.rsqrt(var + LN_EPS) * g + b


def _cache_bf16(pairs):
    @pl.when(pl.program_id(1) == 0)
    def _():
        for w_ref, wb_ref in pairs:
            def body(r, carry, w_ref=w_ref, wb_ref=wb_ref):
                r0 = pl.multiple_of(r * CAST_ROWS, CAST_ROWS)
                wb_ref[pl.ds(r0, CAST_ROWS), :] = w_ref[pl.ds(r0, CAST_ROWS), :].astype(BF16)
                return carry

            lax.fori_loop(0, w_ref.shape[0] // CAST_ROWS, body, 0)


def _wspec(k, tn, layer, col_tile0):
    return pl.BlockSpec((None, k, tn), lambda j, i: (layer, 0, col_tile0 + j))


def _wtspec(k, tn, layer, row_tile0):
    return pl.BlockSpec((None, tn, k), lambda j, i: (layer, row_tile0 + j, 0))


def _bspec(tn, layer, col_tile0):
    return pl.BlockSpec((None, 1, tn), lambda j, i: (layer, 0, col_tile0 + j))


def _glu_kernel(x_ref, wa_ref, wb_ref, ba_ref, bb_ref, o_ref, wab_ref, wbb_ref):
    _cache_bf16([(wa_ref, wab_ref), (wb_ref, wbb_ref)])
    for rows in _row_parts(x_ref.shape[0]):
        x = x_ref[rows, :]
        a = _dot_nt(x, wab_ref[...]) + ba_ref[...]
        b = _dot_nt(x, wbb_ref[...]) + bb_ref[...]
        o_ref[rows, :] = a * _sigmoid(b)


def _glu_proj(xb, wt, b, layer, *, n, tm, tn):
    m, k = xb.shape
    return pl.pallas_call(
        _glu_kernel,
        grid=(n // tn, m // tm),
        in_specs=[
            pl.BlockSpec((tm, k), lambda j, i: (i, 0)),
            _wtspec(k, tn, layer, 0), _wtspec(k, tn, layer, n // tn),
            _bspec(tn, layer, 0), _bspec(tn, layer, n // tn),
        ],
        out_specs=pl.BlockSpec((tm, tn), lambda j, i: (i, j)),
        out_shape=jax.ShapeDtypeStruct((m, n), F32),
        scratch_shapes=[pltpu.VMEM((tn, k), BF16), pltpu.VMEM((tn, k), BF16)],
        compiler_params=_cparams("parallel", "arbitrary"),
        name="glu_proj",
    )(xb, wt, wt, b, b)


def _row_parts(tm):
    return [slice(r, r + ROW_PART) for r in range(0, tm, ROW_PART)]


def _act_proj_kernel(x_ref, w_ref, b_ref, o_ref, wb_ref, *, act):
    _cache_bf16([(w_ref, wb_ref)])
    for rows in _row_parts(x_ref.shape[0]):
        acc = _dot_nt(x_ref[rows, :], wb_ref[...]) + b_ref[...]
        o_ref[rows, :] = act(acc).astype(o_ref.dtype)


def _shifted_proj_kernel(x_ref, w_ref, wn_ref, b_ref, bn_ref, o_ref, wb_ref, bs_ref, *, shift, act):
    tn = w_ref.shape[0]

    @pl.when(pl.program_id(1) == 0)
    def _():
        wb_ref[0:tn - shift, :] = w_ref[shift:tn, :].astype(BF16)
        wb_ref[tn - shift:tn, :] = wn_ref[...].astype(BF16)
        bs_ref[...] = jnp.concatenate([b_ref[...], bn_ref[...]], axis=1)[:, shift:shift + tn]

    for rows in _row_parts(x_ref.shape[0]):
        acc = _dot_nt(x_ref[rows, :], wb_ref[...]) + bs_ref[...]
        o_ref[rows, :] = act(acc).astype(o_ref.dtype)


def _shifted_proj(xb, wt, b, layer, *, col0, n, act, tm, tn, name):
    m, k = xb.shape
    shift = col0 % LANES
    t0 = (col0 - shift) // tn
    return pl.pallas_call(
        functools.partial(_shifted_proj_kernel, shift=shift, act=act),
        grid=(n // tn, m // tm),
        in_specs=[
            pl.BlockSpec((tm, k), lambda j, i: (i, 0)),
            _wtspec(k, tn, layer, t0),
            pl.BlockSpec((None, shift, k), lambda j, i: (layer, (t0 + j + 1) * (tn // shift), 0)),
            _bspec(tn, layer, t0),
            pl.BlockSpec((None, 1, LANES), lambda j, i: (layer, 0, (t0 + j + 1) * (tn // LANES))),
        ],
        out_specs=pl.BlockSpec((tm, tn), lambda j, i: (i, j)),
        out_shape=jax.ShapeDtypeStruct((m, n), BF16),
        scratch_shapes=[pltpu.VMEM((tn, k), BF16), pltpu.VMEM((1, tn), F32)],
        compiler_params=_cparams("parallel", "arbitrary"),
        name=name,
    )(xb, wt, wt, b, b)


def _qkv_kernel(x_ref, w_ref, b_ref, cw_ref, o_ref, wb_ref, tail_ref, *, n_q, n_qk, q_scale):
    _cache_bf16([(w_ref, wb_ref)])
    j = pl.program_id(0)
    tn = w_ref.shape[0]

    @pl.when(pl.program_id(1) == 0)
    def _():
        tail_ref[...] = jnp.zeros(tail_ref.shape, F32)

    is_qk = j < n_qk
    scale = jnp.where(j < n_q, q_scale, 1.0).astype(F32)
    prev = tail_ref[...]
    half_cw = 0.5 * cw_ref[...]
    parts = _row_parts(x_ref.shape[0])
    dots = [_dot_nt(x_ref[parts[0], :], wb_ref[...])]
    for p, rows in enumerate(parts):
        if p + 1 < len(parts):
            dots.append(_dot_nt(x_ref[parts[p + 1], :], wb_ref[...]))
        acc = dots[p] + b_ref[...]
        nr = acc.shape[0]
        full = jnp.concatenate([prev, acc], axis=0)
        prev = acc[nr - SUBLANES:]
        hy = acc * half_cw[SHORT_CONV - 1:SHORT_CONV, :]
        for t in range(SHORT_CONV - 1):
            shifted = pltpu.roll(full, SHORT_CONV - 1 - t, 0)[SUBLANES:]
            hy = hy + shifted * half_cw[t:t + 1, :]
        y = hy * jnp.tanh(hy) + hy
        for h in range(tn // HEAD_DIM):
            cols = slice(h * HEAD_DIM, (h + 1) * HEAD_DIM)
            yh = y[:, cols]
            inv = lax.rsqrt(jnp.sum(yh * yh, axis=-1, keepdims=True) + 1e-6) * scale
            o_ref[rows, cols] = (yh * jnp.where(is_qk, inv, 1.0)).astype(o_ref.dtype)
    tail_ref[...] = prev


def _qkv_proj(xb, wt, b, cw, layer, *, col0, key_dim, n, tm, tn):
    m, k = xb.shape
    return pl.pallas_call(
        functools.partial(_qkv_kernel, n_q=key_dim // tn, n_qk=2 * key_dim // tn,
                          q_scale=HEAD_DIM ** -0.5),
        grid=(n // tn, m // tm),
        in_specs=[
            pl.BlockSpec((tm, k), lambda j, i: (i, 0)),
            _wtspec(k, tn, layer, col0 // tn), _bspec(tn, layer, col0 // tn),
            pl.BlockSpec((None, SHORT_CONV, tn), lambda j, i: (layer, 0, j)),
        ],
        out_specs=pl.BlockSpec((tm, tn), lambda j, i: (i, j)),
        out_shape=jax.ShapeDtypeStruct((m, n), BF16),
        scratch_shapes=[pltpu.VMEM((tn, k), BF16), pltpu.VMEM((SUBLANES, tn), F32)],
        compiler_params=_cparams("parallel", "arbitrary"),
        name="qkv_proj",
    )(xb, wt, b, cw)


def _act_proj(xb, wt, b, layer, *, col0, n, act, tm, tn, name):
    m, k = xb.shape
    return pl.pallas_call(
        functools.partial(_act_proj_kernel, act=act),
        grid=(n // tn, m // tm),
        in_specs=[
            pl.BlockSpec((tm, k), lambda j, i: (i, 0)),
            _wtspec(k, tn, layer, col0 // tn), _bspec(tn, layer, col0 // tn),
        ],
        out_specs=pl.BlockSpec((tm, tn), lambda j, i: (i, j)),
        out_shape=jax.ShapeDtypeStruct((m, n), BF16),
        scratch_shapes=[pltpu.VMEM((tn, k), BF16)],
        compiler_params=_cparams("parallel", "arbitrary"),
        name=name,
    )(xb, wt, b)


def _beta_decay_kernel(x_ref, w_ref, b_ref, a_ref, dt_ref, o_ref, ot_ref, wb_ref, *, n_heads):
    @pl.when(pl.program_id(0) == 0)
    def _():
        wb_ref[...] = jnp.zeros(wb_ref.shape, BF16)
        wb_ref[0:2 * n_heads, :] = w_ref[...].astype(BF16)

    lane = lax.broadcasted_iota(jnp.int32, (x_ref.shape[0], LANES), 1)
    acc = _dot_nt(x_ref[...], wb_ref[...]) + jnp.where(lane < 2 * n_heads, b_ref[...], 0.0)
    beta = _sigmoid(acc)
    g = -jnp.exp(a_ref[...]) * jax.nn.softplus(acc + dt_ref[...])
    pos = lax.broadcasted_iota(jnp.int32, acc.shape, 0) % CHUNK
    shift = 1
    while shift < CHUNK:
        g = g + jnp.where(pos >= shift, pltpu.roll(g, shift, 0), 0.0)
        shift *= 2
    out = jnp.where(lane < n_heads, beta, g)
    o_ref[...] = out
    ot_ref[...] = out.T


def _beta_decay_proj(xb, wt, b, a_log_row, dt_row, layer, *, col0, n_heads, tm):
    m, k = xb.shape
    cblk = col0 // LANES
    return pl.pallas_call(
        functools.partial(_beta_decay_kernel, n_heads=n_heads),
        grid=(m // tm,),
        in_specs=[
            pl.BlockSpec((tm, k), lambda i: (i, 0)),
            pl.BlockSpec((None, 2 * n_heads, k), lambda i: (layer, col0 // (2 * n_heads), 0)),
            pl.BlockSpec((None, 1, LANES), lambda i: (layer, 0, cblk)),
            pl.BlockSpec((None, 1, LANES), lambda i: (layer, 0, 0)),
            pl.BlockSpec((None, 1, LANES), lambda i: (layer, 0, 0)),
        ],
        out_specs=[pl.BlockSpec((tm, LANES), lambda i: (i, 0)),
                   pl.BlockSpec((LANES, tm), lambda i: (0, i))],
        out_shape=[jax.ShapeDtypeStruct((m, LANES), F32), jax.ShapeDtypeStruct((LANES, m), F32)],
        scratch_shapes=[pltpu.VMEM((LANES, k), BF16)],
        compiler_params=_cparams("arbitrary"),
        name="beta_decay_proj",
    )(xb, wt, b, a_log_row, dt_row)


def _conv_ln_kernel(halo_ref, x_ref, w_ref, b_ref, g_ref, bn_ref, o_ref, xs_ref, y_ref, *,
                    ts, halo, rb, ln_rows):
    i = pl.program_id(0)
    ch = xs_ref.shape[1]

    @pl.when(i == 0)
    def _():
        xs_ref[0:halo, :] = jnp.zeros((halo, ch), F32)

    @pl.when(i > 0)
    def _():
        xs_ref[0:halo, :] = halo_ref[...]

    xs_ref[halo:halo + ts, :] = x_ref[...]
    n_rb = ts // rb

    def conv_block(idx, carry):
        c0 = pl.multiple_of((idx // n_rb) * LANES, LANES)
        r0 = (idx % n_rb) * rb
        lanes = pl.ds(c0, LANES)
        acc = jnp.zeros((rb, LANES), F32) + b_ref[:, lanes]
        for r in range(SUBLANES):
            lead = 0 if r == 0 else SUBLANES
            z = None
            for p in range((CONV_WIDTH - 1 - r) // SUBLANES + 1):
                j = CONV_WIDTH - 1 - (SUBLANES * p + r)
                row = pl.multiple_of(r0 + (halo - lead - SUBLANES * p), SUBLANES)
                term = xs_ref[pl.ds(row, rb + lead), lanes] * w_ref[j:j + 1, lanes]
                z = term if z is None else z + term
            acc = acc + (z if r == 0 else z[SUBLANES - r:SUBLANES - r + rb])
        y_ref[pl.ds(pl.multiple_of(r0, rb), rb), lanes] = acc
        return carry

    lax.fori_loop(0, n_rb * (ch // LANES), conv_block, 0)

    def ln_block(r, carry):
        r0 = pl.multiple_of(r * ln_rows, ln_rows)
        y = _layer_norm(y_ref[pl.ds(r0, ln_rows), :], g_ref[...], bn_ref[...])
        o_ref[pl.ds(r0, ln_rows), :] = _silu(y).astype(o_ref.dtype)
        return carry

    lax.fori_loop(0, ts // ln_rows, ln_block, 0)


def _conv_ln(c, w, b, g, bn, *, ts, rb, ln_rows):
    s, ch = c.shape
    halo = 4 * SUBLANES
    return pl.pallas_call(
        functools.partial(_conv_ln_kernel, ts=ts, halo=halo, rb=rb, ln_rows=ln_rows),
        grid=(s // ts,),
        in_specs=[
            pl.BlockSpec((halo, ch), lambda i: (jnp.maximum(i * (ts // halo) - 1, 0), 0)),
            pl.BlockSpec((ts, ch), lambda i: (i, 0)),
            pl.BlockSpec((CONV_WIDTH, ch), lambda i: (0, 0)),
            pl.BlockSpec((1, ch), lambda i: (0, 0)),
            pl.BlockSpec((1, ch), lambda i: (0, 0)),
            pl.BlockSpec((1, ch), lambda i: (0, 0)),
        ],
        out_specs=pl.BlockSpec((ts, ch), lambda i: (i, 0)),
        out_shape=jax.ShapeDtypeStruct((s, ch), BF16),
        scratch_shapes=[pltpu.VMEM((halo + ts, ch), F32), pltpu.VMEM((ts, ch), F32)],
        compiler_params=_cparams("parallel"),
        name="conv_ln",
    )(c, c, w, b, g, bn)


def _gdn_kernel(q_ref, k_ref, v_ref, bd_ref, gt_ref, sz_ref, nw_ref, o_ref, s_ref, *,
                hg, ts, unit, n_heads):
    c = CHUNK
    hgrp = pl.program_id(0)

    @pl.when(pl.program_id(1) == 0)
    def _():
        s_ref[...] = jnp.zeros(s_ref.shape, F32)

    bd = bd_ref[...]
    lane = lax.broadcasted_iota(jnp.int32, (ts, LANES), 1)
    ri = lax.broadcasted_iota(jnp.int32, (unit, unit), 0)
    ci = lax.broadcasted_iota(jnp.int32, (unit, unit), 1)
    same_chunk = (ri // c) == (ci // c)
    causal = same_chunk & (ri >= ci)
    strict = same_chunk & (ri > ci)
    inv_block = (ri // INV_BLOCK) == (ci // INV_BLOCK)
    eye = (ri == ci).astype(F32)
    row_chunk = lax.broadcasted_iota(jnp.int32, (unit, 1), 0) // c
    nw = nw_ref[...]
    n_units = ts // unit
    cpu = unit // c
    cat = jnp.concatenate

    st = []
    for hl in range(hg):
        head = hgrp * hg + hl
        cols = slice(hl * HEAD_DIM, (hl + 1) * HEAD_DIM)
        beta_col = jnp.sum(jnp.where(lane == head, bd, 0.0), axis=1, keepdims=True)
        g_col = jnp.sum(jnp.where(lane == n_heads + head, bd, 0.0), axis=1, keepdims=True)
        g_row = gt_ref[pl.ds(n_heads + head, 1), :]
        for un in range(n_units):
            rows = slice(un * unit, (un + 1) * unit)
            gc, gr, bc = g_col[rows], g_row[:, rows], beta_col[rows]
            decay = jnp.where(causal, jnp.exp(jnp.where(causal, gc - gr, 0.0)), 0.0)
            qbf = q_ref[rows, cols]
            kbf = k_ref[rows, cols]
            k = kbf.astype(F32)
            kb = k * bc
            eg = jnp.exp(gc)
            gram = _dot_nt(cat([_bf(kb), qbf], axis=0), kbf)
            low = jnp.where(strict, gram[:unit] * decay, 0.0)
            dg = jnp.where(inv_block, low, 0.0)
            g_last = [gr[:, cc * c + c - 1:cc * c + c] for cc in range(cpu)]
            g_last_col = g_last[0]
            for cc in range(1, cpu):
                g_last_col = jnp.where(row_chunk == cc, g_last[cc], g_last_col)
            kd = _bf(k * jnp.exp(g_last_col - gc))
            st.append(dict(
                hl=hl, row0=un * unit, attn=_bf(gram[unit:] * decay), dg=dg, n=_bf(low - dg),
                p=eye - dg, rhs=_bf(cat([v_ref[rows, cols].astype(F32) * bc, kb * eg], axis=1)),
                qd=qbf.astype(F32) * eg, gamma=[jnp.exp(gl) for gl in g_last],
                kd=cat([jnp.where(row_chunk == cc, kd, jnp.zeros_like(kd)) for cc in range(cpu)], axis=1)))

    for d in st:
        dgb = _bf(d["dg"])
        d["pw"] = _dot(dgb, dgb)
    for _ in range(2):
        for d in st:
            pwb = _bf(d["pw"])
            r = _dot(cat([pwb, _bf(d["p"])], axis=0), pwb)
            d["pw"], d["p"] = r[:unit], d["p"] + r[unit:]
    for d in st:
        d["p"] = d["p"] + _dot(_bf(d["p"]), _bf(d["pw"]))
    for d in st:
        r = _dot(_bf(d["p"]), cat([d["n"], d["rhs"]], axis=1))
        d["m"], d["y"] = _bf(r[:, :unit]), r[:, unit:]
    for d in st:
        r = _dot(d["m"], cat([d["m"], _bf(d["y"])], axis=1))
        d["m2"], d["z"] = _bf(r[:, :unit]), d["y"] - r[:, unit:]
    for d in st:
        d["uw"] = _bf(d["z"] + _dot(d["m2"], _bf(d["z"])))
    for d in st:
        au = _dot(d["attn"], d["uw"])
        d["o_loc"] = au[:, :HEAD_DIM]
        d["q_eff"] = d["qd"] - au[:, HEAD_DIM:]
        d["bm"] = _dot_tn(d["kd"], d["uw"])

    states = [s_ref[hl] for hl in range(hg)]
    for un in range(n_units):
        for cc in range(cpu):
            for d in st:
                if d["row0"] != un * unit:
                    continue
                hl = d["hl"]
                cols = slice(hl * HEAD_DIM, (hl + 1) * HEAD_DIM)
                bm = d["bm"][cc * HEAD_DIM:(cc + 1) * HEAD_DIM]
                mq = cat([_bf(bm[:, HEAD_DIM:]), _bf(d["q_eff"][cc * c:(cc + 1) * c])], axis=0)
                ms_qs = _dot(mq, _bf(states[hl]))
                o = ms_qs[HEAD_DIM:] + d["o_loc"][cc * c:(cc + 1) * c]
                states[hl] = d["gamma"][cc] * states[hl] - ms_qs[:HEAD_DIM] + bm[:, :HEAD_DIM]
                rows = slice(d["row0"] + cc * c, d["row0"] + (cc + 1) * c)
                o = o * lax.rsqrt(jnp.mean(o * o, axis=-1, keepdims=True) + 1e-6)
                o = o * nw * sz_ref[rows, cols].astype(F32)
                o_ref[rows, cols] = o.astype(o_ref.dtype)
    for hl in range(hg):
        s_ref[hl] = states[hl]


def _gdn(qkvn, bd, bdt, qkvz, nw, *, n_heads, sz_col0, hg, ts, unit):
    s = qkvn.shape[0]
    width = hg * HEAD_DIM
    nblk = n_heads // hg
    sz0 = sz_col0 // width
    return pl.pallas_call(
        functools.partial(_gdn_kernel, hg=hg, ts=ts, unit=unit, n_heads=n_heads),
        grid=(nblk, s // ts),
        in_specs=[
            pl.BlockSpec((ts, width), lambda h, t: (t, h)),
            pl.BlockSpec((ts, width), lambda h, t: (t, nblk + h)),
            pl.BlockSpec((ts, width), lambda h, t: (t, 2 * nblk + h)),
            pl.BlockSpec((ts, LANES), lambda h, t: (t, 0)),
            pl.BlockSpec((LANES, ts), lambda h, t: (0, t)),
            pl.BlockSpec((ts, width), lambda h, t: (t, sz0 + h)),
            pl.BlockSpec((1, HEAD_DIM), lambda h, t: (0, 0)),
        ],
        out_specs=pl.BlockSpec((ts, width), lambda h, t: (t, h)),
        out_shape=jax.ShapeDtypeStruct((s, n_heads * HEAD_DIM), BF16),
        scratch_shapes=[pltpu.VMEM((hg, HEAD_DIM, HEAD_DIM), F32)],
        compiler_params=_cparams("parallel", "arbitrary"),
        name="gated_delta_rule",
    )(qkvn, qkvn, qkvn, bd, bdt, qkvz, nw)


def _merge_kernel(c_ref, og_ref, wc_ref, wg_ref, bc_ref, ga_ref, gb_ref, m_ref, wcb_ref, wgb_ref):
    _cache_bf16([(wc_ref, wcb_ref), (wg_ref, wgb_ref)])
    for rows in _row_parts(c_ref.shape[0]):
        yc = _dot(c_ref[rows, :], wcb_ref[...]) + bc_ref[...]
        yg = _dot(og_ref[rows, :], wgb_ref[...])
        m_ref[rows, :] = (ga_ref[rows, :] * yc + gb_ref[rows, :] * yg).astype(m_ref.dtype)


def _merge_proj(c2, og, wc, wg, bc, gates, layer, *, tm, tn):
    m, k = c2.shape
    n = wc.shape[2]
    return pl.pallas_call(
        _merge_kernel,
        grid=(n // tn, m // tm),
        in_specs=[
            pl.BlockSpec((tm, k), lambda j, i: (i, 0)),
            pl.BlockSpec((tm, k), lambda j, i: (i, 0)),
            _wspec(k, tn, layer, 0), _wspec(k, tn, layer, 0), _bspec(tn, layer, 0),
            pl.BlockSpec((tm, tn), lambda j, i: (i, j)),
            pl.BlockSpec((tm, tn), lambda j, i: (i, n // tn + j)),
        ],
        out_specs=pl.BlockSpec((tm, tn), lambda j, i: (i, j)),
        out_shape=jax.ShapeDtypeStruct((m, n), BF16),
        scratch_shapes=[pltpu.VMEM((k, tn), BF16), pltpu.VMEM((k, tn), BF16)],
        compiler_params=_cparams("parallel", "arbitrary"),
        name="merge_proj",
    )(c2, og, wc, wg, bc, gates, gates)


def _proj_ln_kernel(a_ref, w_ref, res_ref, g_ref, b_ref, o_ref, ob_ref, acc_ref, *, alpha, nk):
    kk = pl.program_id(1)
    parts = _row_parts(a_ref.shape[0])

    def finish(partial_sum):
        for rows in parts:
            acc = partial_sum(rows) + _dot(a_ref[rows, :], w_ref[...])
            y = _layer_norm(acc, g_ref[...], b_ref[...])
            o_ref[rows, :] = y
            ob_ref[rows, :] = y.astype(ob_ref.dtype)

    if nk == 1:
        finish(lambda rows: alpha * res_ref[rows, :])
        return

    @pl.when(kk == 0)
    def _():
        acc_ref[...] = alpha * res_ref[...] + _dot(a_ref[...], w_ref[...])

    @pl.when((kk > 0) & (kk < nk - 1))
    def _():
        acc_ref[...] += _dot(a_ref[...], w_ref[...])

    @pl.when(kk == nk - 1)
    def _():
        finish(lambda rows: acc_ref[rows, :])


def _proj_ln(a, w, res, g, b, layer, *, alpha, tm, tk):
    m, k = a.shape
    n = w.shape[2]
    nk = k // tk
    w_mode = dict(pipeline_mode=pl.Buffered(1)) if nk == 1 else {}
    return pl.pallas_call(
        functools.partial(_proj_ln_kernel, alpha=alpha, nk=nk),
        grid=(m // tm, nk),
        in_specs=[
            pl.BlockSpec((tm, tk), lambda i, kk: (i, kk)),
            pl.BlockSpec((None, tk, n), lambda i, kk: (layer, kk, 0), **w_mode),
            pl.BlockSpec((tm, n), lambda i, kk: (i, 0)),
            pl.BlockSpec((None, 1, n), lambda i, kk: (layer, 0, 0)),
            pl.BlockSpec((None, 1, n), lambda i, kk: (layer, 0, 0)),
        ],
        out_specs=[
            pl.BlockSpec((tm, n), lambda i, kk: (i, 0)),
            pl.BlockSpec((tm, n), lambda i, kk: (i, 0)),
        ],
        out_shape=[jax.ShapeDtypeStruct((m, n), F32), jax.ShapeDtypeStruct((m, n), BF16)],
        scratch_shapes=[pltpu.VMEM((tm if nk > 1 else SUBLANES, n), F32)],
        compiler_params=_cparams("parallel", "arbitrary"),
        name="proj_ln",
    )(a, w, res, g, b)


def _swiglu_in_kernel(x_ref, wg_ref, wu_ref, o_ref, wgb_ref, wub_ref):
    _cache_bf16([(wg_ref, wgb_ref), (wu_ref, wub_ref)])
    for rows in _row_parts(x_ref.shape[0]):
        x = x_ref[rows, :]
        gate = _dot(x, wgb_ref[...])
        up = _dot(x, wub_ref[...])
        o_ref[rows, :] = (_silu(gate) * up).astype(o_ref.dtype)


def _swiglu_in(xb, w, layer, *, tm, tn):
    m, k = xb.shape
    d_ff = w.shape[2] // 2
    return pl.pallas_call(
        _swiglu_in_kernel,
        grid=(d_ff // tn, m // tm),
        in_specs=[
            pl.BlockSpec((tm, k), lambda j, i: (i, 0)),
            _wspec(k, tn, layer, 0), _wspec(k, tn, layer, d_ff // tn),
        ],
        out_specs=pl.BlockSpec((tm, tn), lambda j, i: (i, j)),
        out_shape=jax.ShapeDtypeStruct((m, d_ff), BF16),
        scratch_shapes=[pltpu.VMEM((k, tn), BF16), pltpu.VMEM((k, tn), BF16)],
        compiler_params=_cparams("parallel", "arbitrary"),
        name="swiglu_in",
    )(xb, w, w)


def _rows3(v):
    return v.reshape(v.shape[0], 1, v.shape[1]).astype(F32)


def _lane_row(v, offset):
    depth, n = v.shape
    return jnp.zeros((depth, 1, LANES), F32).at[:, 0, offset:offset + n].set(v.astype(F32))


TM = 1024
TN_ONE = 1024
TN_TWO = 512
TM_SWIGLU = 2048
TM_LN = 512
TM_FFN_OUT = 256
TS_CONV, RB_CONV, LN_ROWS_CONV = 512, 256, 128
GDN_HEADS, TS_GDN, GDN_UNIT = 8, 256, 128


def _layer(x, xb, p, layer, *, alpha):
    d_model = x.shape[1]
    n_heads = p["n_heads"]
    key_dim = n_heads * HEAD_DIM
    o_qkv = 2 * d_model
    o_z = o_qkv + 3 * key_dim
    o_beta = o_z + key_dim
    o_gate = o_beta + 2 * n_heads
    tm = min(TM, x.shape[0])
    tm_ln = min(TM_LN, x.shape[0])

    c = _glu_proj(xb, p["w_in"], p["b_in"], layer, n=d_model, tm=tm, tn=TN_TWO)
    c2 = _conv_ln(c, p["conv_dw_w"][layer], p["conv_dw_b"][layer], p["conv_ln_g"][layer],
                  p["conv_ln_b"][layer], ts=TS_CONV, rb=RB_CONV, ln_rows=LN_ROWS_CONV)

    qkvn = _qkv_proj(xb, p["w_in"], p["b_in"], p["short_conv_w"], layer, col0=o_qkv,
                     key_dim=key_dim, n=3 * key_dim, tm=tm, tn=TN_ONE)
    sz = _act_proj(xb, p["w_in"], p["b_in"], layer, col0=o_z, n=key_dim, act=_silu,
                   tm=tm, tn=TN_ONE, name="z_proj")
    gates = _shifted_proj(xb, p["w_in"], p["b_in"], layer, col0=o_gate, n=2 * d_model,
                          act=_sigmoid, tm=tm, tn=TN_ONE, name="gates_proj")
    bd, bdt = _beta_decay_proj(xb, p["w_in"], p["b_in"], p["a_log_row"], p["dt_row"], layer,
                               col0=o_beta, n_heads=n_heads, tm=tm)
    og = _gdn(qkvn, bd, bdt, sz, p["gdn_norm_w"][layer], n_heads=n_heads, sz_col0=0,
              hg=GDN_HEADS, ts=TS_GDN, unit=GDN_UNIT)

    m = _merge_proj(c2, og, p["w_conv_proj"], p["w_gdn_proj"], p["b_conv_proj"], gates, layer,
                    tm=tm, tn=TN_TWO)
    x1, x1b = _proj_ln(m, p["w_out"], x, p["ln1_g"], p["ln1_b"], layer, alpha=alpha, tm=tm_ln,
                       tk=d_model)
    hff = _swiglu_in(x1b, p["w_ffn_in"], layer, tm=min(TM_SWIGLU, x.shape[0]), tn=TN_TWO)
    x2, x2b = _proj_ln(hff, p["w_ffn_out"], x1, p["ln2_g"], p["ln2_b"], layer, alpha=alpha,
                       tm=min(TM_FFN_OUT, x.shape[0]), tk=hff.shape[1])
    return x2, x2b


def kernel(x, w_in, b_in, conv_dw_w, conv_dw_b, conv_ln_g, conv_ln_b, w_conv_proj, b_conv_proj,
           short_conv_w, a_log, dt_bias, gdn_norm_w, w_gdn_proj, w_out, ln1_g, ln1_b,
           w_ffn_in, w_ffn_out, ln2_g, ln2_b):
    bsz, seq, d_model = x.shape
    depth, n_heads = a_log.shape
    alpha = (2.0 * depth) ** 0.25
    p = dict(
        n_heads=n_heads, w_in=jnp.swapaxes(w_in, 1, 2), b_in=_rows3(b_in),
        a_log_row=_lane_row(a_log, n_heads), dt_row=_lane_row(dt_bias, n_heads),
        conv_dw_w=conv_dw_w, conv_dw_b=_rows3(conv_dw_b), conv_ln_g=_rows3(conv_ln_g),
        conv_ln_b=_rows3(conv_ln_b), w_conv_proj=w_conv_proj, b_conv_proj=_rows3(b_conv_proj),
        short_conv_w=short_conv_w, gdn_norm_w=_rows3(gdn_norm_w),
        w_gdn_proj=w_gdn_proj, w_out=w_out.astype(BF16), ln1_g=_rows3(ln1_g), ln1_b=_rows3(ln1_b),
        w_ffn_in=w_ffn_in, w_ffn_out=w_ffn_out.astype(BF16), ln2_g=_rows3(ln2_g), ln2_b=_rows3(ln2_b))
    outs = []
    for bi in range(bsz):
        h = x[bi]
        hb = h.astype(BF16)
        for layer in range(depth):
            h, hb = _layer(h, hb, p, layer, alpha=alpha)
        outs.append(h)
    return jnp.stack(outs, axis=0)
```
